```python
import math
import jax, jax.numpy as jnp
from jax import lax
import numpy as np

D_MODEL = 1024
BATCH = 8
SEQ = 2048
DEPTH = 2

N_EVEN = (DEPTH + 1) // 2
N_ODD = DEPTH // 2
NORM_EPS = 1e-6

S5_WIDTH = D_MODEL // 2
S5_GROUP = 16
S5_GROUPS = S5_WIDTH // S5_GROUP
S5_STATE = 64
S5_DT_MIN = 1e-3
S5_DT_MAX = 1e-1

HG_WIDTH = D_MODEL // 2
HG_HEADS = 4
HG_KDIM = HG_WIDTH // HG_HEADS
HG_VDIM = HG_WIDTH // HG_HEADS
HG_CHUNK = 64

IN0_COLS = S5_WIDTH + 2 * HG_HEADS * HG_KDIM + 2 * HG_WIDTH

ATT_HEAD_DIM = 64
ATT_HEADS_PER_GROUP = 8
ATT_BRANCHES = ((128, 1), (512, 4), (2048, 16))
N_BRANCH = len(ATT_BRANCHES)
ATT_BLOCK = 128
ATT_GROUP_WIDTH = ATT_HEADS_PER_GROUP * ATT_HEAD_DIM
IN1_COLS = 3 * N_BRANCH * ATT_GROUP_WIDTH
ROT_DIM = ATT_HEAD_DIM // 4
ROPE_THETA = 500000.0

D_FF = 2816
CONV_W = 3

kernel_name = "hybrid_s5_hgrn2_dilated_convffn"


def _rmsnorm(x, g):
    xf = x.astype(jnp.float32)
    y = xf * lax.rsqrt(jnp.mean(xf * xf, axis=-1, keepdims=True) + NORM_EPS)
    return (y * g.astype(jnp.float32)).astype(x.dtype)


def _cplx_scan_op(e1, e2):
    a1r, a1i, b1r, b1i = e1
    a2r, a2i, b2r, b2i = e2
    ar = a2r * a1r - a2i * a1i
    ai = a2r * a1i + a2i * a1r
    br = a2r * b1r - a2i * b1i + b2r
    bi = a2r * b1i + a2i * b1r + b2i
    return (ar, ai, br, bi)


def _s5_mixer(u, A_re, A_im, log_dt, B_re, B_im, C_re, C_im, Dd, glu_w, glu_b):
    Bsz, L, _ = u.shape
    f32 = jnp.float32
    A_re, A_im = A_re.astype(f32), A_im.astype(f32)
    B_re, B_im = B_re.astype(f32), B_im.astype(f32)
    C_re, C_im = C_re.astype(f32), C_im.astype(f32)
    ug = u.astype(f32).reshape(Bsz, L, S5_GROUPS, S5_GROUP)
    dt = jnp.exp(log_dt.astype(f32))[:, None]
    mag = jnp.exp(A_re * dt)
    ab_re = mag * jnp.cos(A_im * dt)
    ab_im = mag * jnp.sin(A_im * dt)
    den = A_re * A_re + A_im * A_im
    nr, ni = ab_re - 1.0, ab_im
    c_re = (nr * A_re + ni * A_im) / den
    c_im = (ni * A_re - nr * A_im) / den
    Bb_re = c_re[..., None] * B_re - c_im[..., None] * B_im
    Bb_im = c_re[..., None] * B_im + c_im[..., None] * B_re
    bu_re = jnp.einsum('gpc,blgc->blgp', Bb_re, ug)
    bu_im = jnp.einsum('gpc,blgc->blgp', Bb_im, ug)
    a_re = jnp.broadcast_to(ab_re[None, None], (1, L, S5_GROUPS, S5_STATE))
    a_im = jnp.broadcast_to(ab_im[None, None], (1, L, S5_GROUPS, S5_STATE))
    _, _, x_re, x_im = lax.associative_scan(_cplx_scan_op, (a_re, a_im, bu_re, bu_im), axis=1)
    y = (jnp.einsum('gcp,blgp->blgc', C_re, x_re)
         - jnp.einsum('gcp,blgp->blgc', C_im, x_im)
         + Dd.astype(f32) * ug)
    z = jax.nn.gelu(y.reshape(Bsz, L, S5_WIDTH))
    return z * jax.nn.sigmoid(z @ glu_w.astype(f32) + glu_b.astype(f32))


def _hgrn2_mixer(xq, xf, xi, xg, lb, norm_g):
    Bsz, L, _ = xq.shape
    f32 = jnp.float32
    nc = L // HG_CHUNK
    q = jax.nn.silu(xq.astype(f32)).reshape(Bsz, L, HG_HEADS, HG_KDIM)
    f = lb + (1.0 - lb) * jax.nn.sigmoid(xf.astype(f32))
    k = (1.0 - f).reshape(Bsz, L, HG_HEADS, HG_KDIM)
    logf = jnp.log(f).reshape(Bsz, L, HG_HEADS, HG_KDIM)
    v = xi.astype(f32).reshape(Bsz, L, HG_HEADS, HG_VDIM)

    def to_chunks(t):
        return t.reshape(Bsz, nc, HG_CHUNK, HG_HEADS, -1).transpose(1, 0, 3, 2, 4)

    causal = jnp.tril(jnp.ones((HG_CHUNK, HG_CHUNK), dtype=bool))[None, None, :, :, None]

    def step(S, inp):
        qc, kc, gc, vc = inp
        b = jnp.cumsum(gc, axis=2)
        o_inter = jnp.einsum('bhtk,bhkv->bhtv', qc * jnp.exp(b), S)
        diff = b[:, :, :, None, :] - b[:, :, None, :, :]
        decay = jnp.exp(jnp.where(causal, diff, -jnp.inf))
        att = jnp.einsum('bhtk,bhsk,bhtsk->bhts', qc, kc, decay)
        o_intra = jnp.einsum('bhts,bhsv->bhtv', att, vc)
        b_last = b[:, :, -1, :]
        S_new = (jnp.exp(b_last)[..., None] * S
                 + jnp.einsum('bhsk,bhsv->bhkv', kc * jnp.exp(b_last[:, :, None, :] - b), vc))
        return S_new, o_inter + o_intra

    S0 = jnp.zeros((Bsz, HG_HEADS, HG_KDIM, HG_VDIM), f32)
    _, o = lax.scan(step, S0, (to_chunks(q), to_chunks(k), to_chunks(logf), to_chunks(v)))
    o = o.transpose(1, 0, 3, 2, 4).reshape(Bsz, L, HG_HEADS, HG_VDIM)
    o = o * lax.rsqrt(jnp.mean(o * o, axis=-1, keepdims=True) + NORM_EPS)
    o = o * norm_g.astype(f32).reshape(HG_HEADS, HG_VDIM)
    return o.reshape(Bsz, L, HG_WIDTH) * jax.nn.silu(xg.astype(f32))


def _partial_rotary(t, positions):
    half = ROT_DIM // 2
    inv_freq = ROPE_THETA ** (-jnp.arange(half, dtype=jnp.float32) * 2.0 / ROT_DIM)
    ang = positions.astype(jnp.float32)[..., None] * inv_freq
    cos = jnp.cos(ang)[:, :, None, :]
    sin = jnp.sin(ang)[:, :, None, :]
    x1 = t[..., :half]
    x2 = t[..., half:ROT_DIM]
    return jnp.concatenate([x1 * cos - x2 * sin, x2 * cos + x1 * sin, t[..., ROT_DIM:]], axis=-1)


def _dilated_branch(q, k, v, dil, steps):
    Bsz, L, H, E = q.shape
    M = L // dil
    nb = -(-M // ATT_BLOCK)
    Mp = nb * ATT_BLOCK

    def to_blocks(t):
        t = t.reshape(Bsz, M, dil, H, E).transpose(0, 2, 3, 1, 4)
        t = jnp.pad(t, ((0, 0), (0, 0), (0, 0), (0, Mp - M), (0, 0)))
        return t.reshape(Bsz, dil, H, nb, ATT_BLOCK, E)

    def with_prev(t):
        prev = jnp.pad(t[:, :, :, :-1], ((0, 0), (0, 0), (0, 0), (1, 0), (0, 0), (0, 0)))
        return jnp.concatenate([prev, t], axis=-2)

    qb = to_blocks(q) * (E ** -0.5)
    kc = with_prev(to_blocks(k))
    vc = with_prev(to_blocks(v))
    s = jnp.einsum('bdhnqe,bdhnke->bdhnqk', qb, kc)
    qi = jnp.arange(ATT_BLOCK)[:, None] + ATT_BLOCK
    kj = jnp.arange(2 * ATT_BLOCK)[None, :]
    back = qi - kj
    in_range = (jnp.arange(nb)[:, None, None] * ATT_BLOCK - ATT_BLOCK + kj[None]) >= 0
    valid = (back >= 0) & (back <= steps) & in_range
    s = jnp.where(valid, s, -jnp.inf)
    m = jnp.max(s, axis=-1, keepdims=True)
    p = jnp.exp(s - m)
    den = jnp.sum(p, axis=-1, keepdims=True)
    o = jnp.einsum('bdhnqk,bdhnke->bdhnqe', p, vc) / den
    lse = (m + jnp.log(den))[..., 0]
    o = o.reshape(Bsz, dil, H, Mp, E)[:, :, :, :M].transpose(0, 3, 1, 2, 4).reshape(Bsz, L, H, E)
    lse = lse.reshape(Bsz, dil, H, Mp)[..., :M].transpose(0, 3, 1, 2).reshape(Bsz, L, H)
    return o, lse


def _dilated_attention(h, positions, w_qkv, w_o):
    Bsz, L, _ = h.shape
    f32 = jnp.float32
    qkv = (h @ w_qkv).astype(f32).reshape(Bsz, L, 3, N_BRANCH * ATT_HEADS_PER_GROUP, ATT_HEAD_DIM)
    q = _partial_rotary(qkv[:, :, 0], positions)
    k = _partial_rotary(qkv[:, :, 1], positions)
    v = qkv[:, :, 2]
    outs, lses = [], []
    for g, (win, dil) in enumerate(ATT_BRANCHES):
        sl = slice(g * ATT_HEADS_PER_GROUP, (g + 1) * ATT_HEADS_PER_GROUP)
        o_g, lse_g = _dilated_branch(q[:, :, sl], k[:, :, sl], v[:, :, sl], dil, win // dil)
        outs.append(o_g)
        lses.append(lse_g)
    alpha = jax.nn.softmax(jnp.stack(lses, axis=0), axis=0)
    o = jnp.sum(alpha[..., None] * jnp.stack(outs, axis=0), axis=0)
    return o.reshape(Bsz, L, ATT_GROUP_WIDTH).astype(h.dtype) @ w_o


def _conv_ffn(h, w_in, conv_w, conv_b, w_out):
    hu = h @ w_in
    C = hu.shape[-1]
    hu = lax.conv_general_dilated(
        hu, conv_w.astype(hu.dtype)[:, None, :], window_strides=(1,),
        padding=[(CONV_W - 1, 0)], dimension_numbers=('NWC', 'WIO', 'NWC'),
        feature_group_count=C) + conv_b.astype(hu.dtype)
    a, b = hu[..., :D_FF], hu[..., D_FF:]
    return (jax.nn.silu(a) * b) @ w_out


def setup_inputs(seed: int = 0) -> dict:
    key = jax.random.key(seed)
    ks = jax.random.split(key, 24)
    f32 = jnp.float32

    def nrm(k, shape, scale):
        return jax.random.normal(k, shape, f32) * scale

    mix_width0 = S5_WIDTH + HG_WIDTH
    n_idx = jnp.arange(S5_STATE, dtype=f32)
    return {
        "x": nrm(ks[0], (BATCH, SEQ, D_MODEL), 1.0),
        "positions": jnp.broadcast_to(jnp.arange(SEQ, dtype=jnp.int32), (BATCH, SEQ)),
        "norm_mix": 1.0 + nrm(ks[1], (DEPTH, D_MODEL), 0.02),
        "norm_ffn": 1.0 + nrm(ks[2], (DEPTH, D_MODEL), 0.02),
        "norm_final": 1.0 + nrm(ks[3], (D_MODEL,), 0.02),
        "mix_w_in": nrm(ks[4], (N_EVEN, D_MODEL, IN0_COLS), D_MODEL ** -0.5),
        "mix_w_out": nrm(ks[5], (N_EVEN, mix_width0, D_MODEL), mix_width0 ** -0.5),
        "s5_A_re": -0.5 + nrm(ks[6], (N_EVEN, S5_GROUPS, S5_STATE), 0.01),
        "s5_A_im": math.pi * n_idx + nrm(ks[7], (N_EVEN, S5_GROUPS, S5_STATE), 0.01),
        "s5_log_dt": jax.random.uniform(ks[8], (N_EVEN, S5_GROUPS), f32,
                                        math.log(S5_DT_MIN), math.log(S5_DT_MAX)),
        "s5_B_re": nrm(ks[9], (N_EVEN, S5_GROUPS, S5_STATE, S5_GROUP), (2 * S5_GROUP) ** -0.5),
        "s5_B_im": nrm(ks[10], (N_EVEN, S5_GROUPS, S5_STATE, S5_GROUP), (2 * S5_GROUP) ** -0.5),
        "s5_C_re": nrm(ks[11], (N_EVEN, S5_GROUPS, S5_GROUP, S5_STATE), (2 * S5_STATE) ** -0.5),
        "s5_C_im": nrm(ks[12], (N_EVEN, S5_GROUPS, S5_GROUP, S5_STATE), (2 * S5_STATE) ** -0.5),
        "s5_D": nrm(ks[13], (N_EVEN, S5_GROUPS, S5_GROUP), 1.0),
        "s5_glu_w": nrm(ks[14], (N_EVEN, S5_WIDTH, S5_WIDTH), S5_WIDTH ** -0.5),
        "s5_glu_b": nrm(ks[15], (N_EVEN, S5_WIDTH), 0.01),
        "hgrn_gamma": nrm(ks[16], (N_EVEN + 1, HG_HEADS * HG_KDIM), 0.1),
        "hgrn_norm": 1.0 + nrm(ks[17], (N_EVEN, HG_WIDTH), 0.02),
        "att_w_qkv": nrm(ks[18], (N_ODD, D_MODEL, IN1_COLS), D_MODEL ** -0.5),
        "att_w_o": nrm(ks[19], (N_ODD, ATT_GROUP_WIDTH, D_MODEL), ATT_GROUP_WIDTH ** -0.5),
        "ffn_w_in": nrm(ks[20], (DEPTH, D_MODEL, 2 * D_FF), D_MODEL ** -0.5),
        "ffn_conv_w": nrm(ks[21], (DEPTH, CONV_W, 2 * D_FF), CONV_W ** -0.5),
        "ffn_conv_b": nrm(ks[22], (DEPTH, 2 * D_FF), 0.01),
        "ffn_w_out": nrm(ks[23], (DEPTH, D_FF, D_MODEL), D_FF ** -0.5),
    }


def reference(x, positions, norm_mix, norm_ffn, norm_final, mix_w_in, mix_w_out,
              s5_A_re, s5_A_im, s5_log_dt, s5_B_re, s5_B_im, s5_C_re, s5_C_im, s5_D,
              s5_glu_w, s5_glu_b, hgrn_gamma, hgrn_norm, att_w_qkv, att_w_o,
              ffn_w_in, ffn_conv_w, ffn_conv_b, ffn_w_out):
    lb_all = jnp.cumsum(jax.nn.softmax(hgrn_gamma.astype(jnp.float32), axis=0), axis=0)
    h = x
    c_q = S5_WIDTH
    c_f = c_q + HG_HEADS * HG_KDIM
    c_i = c_f + HG_HEADS * HG_KDIM
    c_g = c_i + HG_WIDTH
    for layer in range(DEPTH):
        hn = _rmsnorm(h, norm_mix[layer])
        j = layer // 2
        if layer % 2 == 0:
            proj = hn @ mix_w_in[j]
            oa = _s5_mixer(proj[..., :c_q], s5_A_re[j], s5_A_im[j], s5_log_dt[j],
                           s5_B_re[j], s5_B_im[j], s5_C_re[j], s5_C_im[j], s5_D[j],
                           s5_glu_w[j], s5_glu_b[j])
            ob = _hgrn2_mixer(proj[..., c_q:c_f], proj[..., c_f:c_i], proj[..., c_i:c_g],
                              proj[..., c_g:], lb_all[j], hgrn_norm[j])
            mix = jnp.concatenate([oa, ob], axis=-1).astype(h.dtype) @ mix_w_out[j]
        else:
            mix = _dilated_attention(hn, positions, att_w_qkv[j], att_w_o[j])
        h = h + mix.astype(h.dtype)
        ff = _conv_ffn(_rmsnorm(h, norm_ffn[layer]), ffn_w_in[layer], ffn_conv_w[layer],
                       ffn_conv_b[layer], ffn_w_out[layer])
        h = h + ff.astype(h.dtype)
    return _rmsnorm(h, norm_final)
```

```python
import functools
import math

import jax
import jax.numpy as jnp
from jax import lax
from jax.experimental import pallas as pl
from jax.experimental.pallas import tpu as pltpu

F32 = jnp.float32
BF16 = jnp.bfloat16

NORM_EPS = 1e-6
S5_GROUP = 16
S5_STATE = 64
HG_HEADS = 4
HG_BLOCK = 16
ATT_HEAD_DIM = 64
ATT_BLOCK = 128
ATT_DILATIONS = (1, 4, 16)
ROT_DIM = 16
ROPE_THETA = 500000.0
CONV_W = 3
NEG_BIG = -1e30

VMEM_LIMIT = 48 * 1024 * 1024


def _cparams(n_axes):
    return pltpu.CompilerParams(dimension_semantics=("arbitrary",) * n_axes,
                                vmem_limit_bytes=VMEM_LIMIT)


def _rms(x, g):
    return x * lax.rsqrt(jnp.mean(x * x, axis=-1, keepdims=True) + NORM_EPS) * g


def _sigmoid(x):
    return 1.0 / (1.0 + jnp.exp(-x))


def _dot(a, b):
    return jnp.dot(a, b, preferred_element_type=F32)


def _dot_nt(a, b):
    return lax.dot_general(a, b, (((1,), (1,)), ((), ())), preferred_element_type=F32)


def _dot_tn(a, b):
    return lax.dot_general(a, b, (((0,), (0,)), ((), ())), preferred_element_type=F32)


def _inproj_kernel(x_ref, g_ref, w_ref, u_ref, hg_ref, xn_ref):
    j = pl.program_id(1)

    @pl.when(j == 0)
    def _():
        xn_ref[...] = _rms(x_ref[...], g_ref[...]).astype(BF16)
        u_ref[...] = _dot(xn_ref[...], w_ref[...])

    @pl.when(j > 0)
    def _():
        hg_ref[...] = _dot(xn_ref[...], w_ref[...])


def _inproj(x2, g, w_bf, B, L, tm=512, tn=512):
    N, D = x2.shape
    ncol = w_bf.shape[1] // tn
    tps = L // tm
    return pl.pallas_call(
        _inproj_kernel,
        out_shape=(jax.ShapeDtypeStruct((L, B * tn), F32),
                   jax.ShapeDtypeStruct((N, (ncol - 1) * tn), F32)),
        grid=(N // tm, ncol),
        in_specs=[pl.BlockSpec((tm, D), lambda i, j: (i, 0)),
                  pl.BlockSpec((1, D), lambda i, j: (0, 0)),
                  pl.BlockSpec((D, tn), lambda i, j: (0, j))],
        out_specs=(pl.BlockSpec((tm, tn), lambda i, j: (i % tps, i // tps)),
                   pl.BlockSpec((tm, tn), lambda i, j: (i, jnp.maximum(j - 1, 0)))),
        scratch_shapes=[pltpu.VMEM((tm, D), BF16)],
        compiler_params=_cparams(2),
        name="inproj",
    )(x2, g, w_bf)


def _s5_kernel(u_ref, bd0_ref, bd1_ref, ar_ref, ai_ref, c0_ref, c1_ref, d_ref, gw_ref, gb_ref,
               o_ref, x0_ref, x1_ref, st_ref, *, nb, lane_chunk):
    tm = u_ref.shape[0]
    half = x0_ref.shape[1] // 2
    steps = tm // nb

    @pl.when(pl.program_id(0) == 0)
    def _():
        st_ref[...] = jnp.zeros_like(st_ref)

    u = u_ref[...]
    ub = u.astype(BF16)
    ku = ub.shape[1] // 2
    x0_ref[...] = _dot(ub[:, :ku], bd0_ref[...])
    x1_ref[...] = _dot(ub[:, ku:], bd1_ref[...])

    for k, x_ref in enumerate((x0_ref, x1_ref)):
        for c in range(half // lane_chunk):
            lo = c * lane_chunk
            re = slice(lo, lo + lane_chunk)
            im = slice(half + lo, half + lo + lane_chunk)
            a_re = jnp.broadcast_to(ar_ref[:, k * half + lo:k * half + lo + lane_chunk], (nb, lane_chunk))
            a_im = jnp.broadcast_to(ai_ref[:, k * half + lo:k * half + lo + lane_chunk], (nb, lane_chunk))

            def body(s, carry, x_ref=x_ref, re=re, im=im, a_re=a_re, a_im=a_im):
                xr, xi = carry
                rows = pl.ds(pl.multiple_of(s * nb, nb), nb)
                nr = a_re * xr - a_im * xi + x_ref[rows, re]
                ni = a_re * xi + a_im * xr + x_ref[rows, im]
                x_ref[rows, re] = nr
                x_ref[rows, im] = ni
                return nr, ni

            xr, xi = lax.fori_loop(0, steps, body, (st_ref[2 * k, :, re], st_ref[2 * k + 1, :, re]),
                                   unroll=8)
            st_ref[2 * k, :, re] = xr
            st_ref[2 * k + 1, :, re] = xi

    y0 = _dot(x0_ref[...].astype(BF16), c0_ref[...])
    y1 = _dot(x1_ref[...].astype(BF16), c1_ref[...])
    y = jnp.concatenate([y0, y1], axis=1) + d_ref[...] * u
    z = 0.5 * y * (1.0 + jnp.tanh(math.sqrt(2.0 / math.pi) * (y + 0.044715 * (y * y * y))))
    gate = _sigmoid(_dot(z.astype(BF16), gw_ref[...]) + gb_ref[...])
    o_ref[...] = z * gate


def _s5(u_tm, bd0, bd1, ar, ai, c0, c1, dd, gw, gb, B, tm=512):
    R, W = u_tm.shape
    S2 = bd0.shape[1]
    full = lambda a: pl.BlockSpec(a.shape, lambda t: (0,) * a.ndim)
    kern = functools.partial(_s5_kernel, nb=B, lane_chunk=512)
    return pl.pallas_call(
        kern,
        out_shape=jax.ShapeDtypeStruct((R, W), F32),
        grid=(R // tm,),
        in_specs=[pl.BlockSpec((tm, W), lambda t: (t, 0)),
                  full(bd0), full(bd1), full(ar), full(ai), full(c0), full(c1), full(dd), full(gw), full(gb)],
        out_specs=pl.BlockSpec((tm, W), lambda t: (t, 0)),
        scratch_shapes=[pltpu.VMEM((tm, S2), F32), pltpu.VMEM((tm, S2), F32),
                        pltpu.VMEM((4, B, S2 // 2), F32)],
        compiler_params=_cparams(1),
        name="s5_mixer",
    )(u_tm, bd0, bd1, ar, ai, c0, c1, dd, gw, gb)


def _hgrn_kernel(xq_ref, xf_ref, xi_ref, xg_ref, lb_ref, ng_ref, o_ref,
                 st_ref, qe_ref, ke_ref, eb_ref, q_ref, k_ref, b_ref, v_ref, oi_ref):
    tl, W = xq_ref.shape
    C = HG_BLOCK
    hd = W // HG_HEADS
    nblk = tl // C

    @pl.when(pl.program_id(1) == 0)
    def _():
        st_ref[...] = jnp.zeros_like(st_ref)
        k_ref[0:C, :] = jnp.zeros((C, W), F32)
        b_ref[0:C, :] = jnp.zeros((C, W), F32)
        v_ref[0:C, :] = jnp.zeros((C, W), F32)

    xq = xq_ref[...]
    q = xq * _sigmoid(xq)
    lb = lb_ref[...]
    f = lb + (1.0 - lb) * _sigmoid(xf_ref[...])
    kk = 1.0 - f
    g = jnp.log(f)
    v = xi_ref[...]

    row = lax.broadcasted_iota(jnp.int32, (tl, tl), 0)
    col = lax.broadcasted_iota(jnp.int32, (tl, tl), 1)
    same = (row // C) == (col // C)
    t_cum = jnp.where(same & (col <= row), 1.0, 0.0).astype(BF16)
    t_blk = jnp.where(same, 1.0, 0.0).astype(BF16)
    g_hi = g.astype(BF16)
    g_lo = (g - g_hi.astype(F32)).astype(BF16)
    b = _dot(t_cum, g_hi) + _dot(t_cum, g_lo)
    bl = _dot(t_blk, g_hi) + _dot(t_blk, g_lo)

    q_ref[...] = q
    qe_ref[...] = q * jnp.exp(b)
    ke_ref[...] = kk * jnp.exp(bl - b)
    eb_ref[...] = jnp.exp(bl)
    k_ref[C:, :] = kk
    b_ref[C:, :] = b
    v_ref[C:, :] = v

    def blk_body(i, carry):
        rows = pl.ds(pl.multiple_of(i * C, C), C)
        for h in range(HG_HEADS):
            hl = slice(h * hd, (h + 1) * hd)
            st = st_ref[h]
            oi_ref[rows, hl] = _dot_nt(qe_ref[rows, hl].astype(BF16), st.astype(BF16))
            vb = v_ref[pl.ds(pl.multiple_of(i * C, C) + C, C), hl]
            kv = _dot_tn(vb.astype(BF16), ke_ref[rows, hl].astype(BF16))
            st_ref[h] = st * eb_ref[pl.ds(pl.multiple_of(i * C, C), 1), hl] + kv
        return carry

    lax.fori_loop(0, nblk, blk_body, 0)

    rc = 64 if tl % 64 == 0 else tl
    lag_ok = [(lax.broadcasted_iota(jnp.int32, (rc, 1), 0) % C) >= d for d in range(C)]
    for h in range(HG_HEADS):
        hl = slice(h * hd, (h + 1) * hd)
        for r0 in range(0, tl, rc):
            qc = q_ref[r0:r0 + rc, hl]
            bc = b_ref[C + r0:C + r0 + rc, hl]
            acc = oi_ref[r0:r0 + rc, hl]
            for d in range(C):
                s0 = C + r0 - d
                e = qc * k_ref[s0:s0 + rc, hl] * jnp.exp(bc - b_ref[s0:s0 + rc, hl])
                a = jnp.sum(jnp.where(lag_ok[d], e, 0.0), axis=-1, keepdims=True)
                acc = acc + a * v_ref[s0:s0 + rc, hl]
            on = acc * lax.rsqrt(jnp.mean(acc * acc, axis=-1, keepdims=True) + NORM_EPS)
            xg = xg_ref[r0:r0 + rc, hl]
            o_ref[r0:r0 + rc, hl] = on * ng_ref[:, hl] * (xg * _sigmoid(xg))


def _hgrn(hg, lb, ng, B, L, tl=128):
    N = hg.shape[0]
    W = lb.shape[1]
    hd = W // HG_HEADS
    tps = L // tl
    C = HG_BLOCK
    col = lambda c: pl.BlockSpec((tl, W), lambda b, t: (b * tps + t, c))
    return pl.pallas_call(
        _hgrn_kernel,
        out_shape=jax.ShapeDtypeStruct((N, W), F32),
        grid=(B, tps),
        in_specs=[col(0), col(1), col(2), col(3),
                  pl.BlockSpec((1, W), lambda b, t: (0, 0)),
                  pl.BlockSpec((1, W), lambda b, t: (0, 0))],
        out_specs=pl.BlockSpec((tl, W), lambda b, t: (b * tps + t, 0)),
        scratch_shapes=[pltpu.VMEM((HG_HEADS, hd, hd), F32),
                        pltpu.VMEM((tl, W), F32),
                        pltpu.VMEM((tl, W), F32),
                        pltpu.VMEM((tl, W), F32),
                        pltpu.VMEM((tl, W), F32),
                        pltpu.VMEM((tl + C, W), F32),
                        pltpu.VMEM((tl + C, W), F32),
                        pltpu.VMEM((tl + C, W), F32),
                        pltpu.VMEM((tl, W), F32)],
        compiler_params=_cparams(2),
        name="hgrn2_mixer",
    )(hg, hg, hg, hg, lb, ng)


def _proj_resid_kernel(*refs, n_in):
    a_refs = refs[:n_in]
    w_ref, r_ref, o_ref = refs[n_in:]
    acc = r_ref[...]
    k0 = 0
    for a_ref in a_refs:
        kw = a_ref.shape[1]
        acc = acc + _dot(a_ref[...].astype(BF16), w_ref[k0:k0 + kw, :])
        k0 += kw
    o_ref[...] = acc


def _proj_resid(a_list, a_specs, w_bf, resid, tm):
    N, D = resid.shape
    return pl.pallas_call(
        functools.partial(_proj_resid_kernel, n_in=len(a_list)),
        out_shape=jax.ShapeDtypeStruct((N, D), F32),
        grid=(N // tm,),
        in_specs=list(a_specs) + [pl.BlockSpec(w_bf.shape, lambda i: (0, 0)),
                                  pl.BlockSpec((tm, D), lambda i: (i, 0))],
        out_specs=pl.BlockSpec((tm, D), lambda i: (i, 0)),
        compiler_params=_cparams(1),
        name="proj_resid",
    )(*a_list, w_bf, resid)


FFN_HALO = 16


def _ffn_kernel(x_ref, halo_ref, g_ref, wa_ref, wb_ref, cwa_ref, cwb_ref, cba_ref, cbb_ref, wo_ref,
                gf_ref, o_ref, xn_ref, acc_ref, a_ref, b_ref, *, tiles_per_seq, final_norm):
    i = pl.program_id(0)
    j = pl.program_id(1)
    tm = x_ref.shape[0]
    H = FFN_HALO

    @pl.when(j == 0)
    def _():
        xn_ref[H:, :] = _rms(x_ref[...], g_ref[...]).astype(BF16)
        hn = _rms(halo_ref[...], g_ref[...])
        hn = jnp.where(i % tiles_per_seq == 0, 0.0, hn)
        xn_ref[0:H, :] = hn.astype(BF16)
        acc_ref[...] = jnp.zeros_like(acc_ref)

    xn = xn_ref[...]
    a_ref[...] = _dot(xn, wa_ref[...])
    b_ref[...] = _dot(xn, wb_ref[...])

    def conv(s_ref, cw_ref, cb_ref):
        out = cb_ref[...] + s_ref[H:H + tm, :] * cw_ref[CONV_W - 1:CONV_W, :]
        for t in range(1, CONV_W):
            out = out + s_ref[H - t:H - t + tm, :] * cw_ref[CONV_W - 1 - t:CONV_W - t, :]
        return out

    a = conv(a_ref, cwa_ref, cba_ref)
    b = conv(b_ref, cwb_ref, cbb_ref)
    act = a * _sigmoid(a) * b
    acc_ref[...] += _dot(act.astype(BF16), wo_ref[...])

    @pl.when(j == pl.num_programs(1) - 1)
    def _():
        y = x_ref[...] + acc_ref[...]
        if final_norm:
            y = _rms(y, gf_ref[...])
        o_ref[...] = y


def _ffn(h, g, w_in_bf, conv_w, conv_b, w_out_bf, g_final, L, final_norm, tm=512, tf=256):
    N, D = h.shape
    dff = w_out_bf.shape[0]
    nj = dff // tf
    H = FFN_HALO
    kern = functools.partial(_ffn_kernel, tiles_per_seq=L // tm, final_norm=final_norm)
    return pl.pallas_call(
        kern,
        out_shape=jax.ShapeDtypeStruct((N, D), F32),
        grid=(N // tm, nj),
        in_specs=[pl.BlockSpec((tm, D), lambda i, j: (i, 0)),
                  pl.BlockSpec((H, D), lambda i, j: (jnp.maximum(i * (tm // H) - 1, 0), 0)),
                  pl.BlockSpec((1, D), lambda i, j: (0, 0)),
                  pl.BlockSpec((D, tf), lambda i, j: (0, j)),
                  pl.BlockSpec((D, tf), lambda i, j: (0, j + nj)),
                  pl.BlockSpec((CONV_W, tf), lambda i, j: (0, j)),
                  pl.BlockSpec((CONV_W, tf), lambda i, j: (0, j + nj)),
                  pl.BlockSpec((1, tf), lambda i, j: (0, j)),
                  pl.BlockSpec((1, tf), lambda i, j: (0, j + nj)),
                  pl.BlockSpec((tf, D), lambda i, j: (j, 0)),
                  pl.BlockSpec((1, D), lambda i, j: (0, 0))],
        out_specs=pl.BlockSpec((tm, D), lambda i, j: (i, 0)),
        scratch_shapes=[pltpu.VMEM((tm + H, D), BF16),
                        pltpu.VMEM((tm, D), F32),
                        pltpu.VMEM((tm + H, tf), F32),
                        pltpu.VMEM((tm + H, tf), F32)],
        compiler_params=_cparams(2),
        name="conv_ffn",
    )(h, h, g, w_in_bf, w_in_bf, conv_w, conv_w, conv_b, conv_b, w_out_bf, g_final)


def _qkv_kernel(x_ref, g_ref, w_ref, cos_ref, sa_ref, sb_ref, o_ref, xn_ref, *, n_rot_tiles):
    j = pl.program_id(1)

    @pl.when(j == 0)
    def _():
        xn_ref[...] = _rms(x_ref[...], g_ref[...]).astype(BF16)

    y = _dot(xn_ref[...], w_ref[...])

    @pl.when(j < n_rot_tiles)
    def _():
        c = cos_ref[...]
        sa = sa_ref[...]
        sb = sb_ref[...]
        lw = c.shape[1]
        for s in range(y.shape[1] // lw):
            t = y[:, s * lw:(s + 1) * lw]
            half = ROT_DIM // 2
            o_ref[:, s * lw:(s + 1) * lw] = (t * c + pltpu.roll(t, lw - half, axis=1) * sa
                                             + pltpu.roll(t, half, axis=1) * sb)

    @pl.when(j >= n_rot_tiles)
    def _():
        o_ref[...] = y


def _qkv(h, g, w_bf, cos_t, sin_a, sin_b, n_rot_cols, tm=512, tn=512):
    N, D = h.shape
    ncols = w_bf.shape[1]
    lw = cos_t.shape[1]
    return pl.pallas_call(
        functools.partial(_qkv_kernel, n_rot_tiles=n_rot_cols // tn),
        out_shape=jax.ShapeDtypeStruct((N, ncols), F32),
        grid=(N // tm, ncols // tn),
        in_specs=[pl.BlockSpec((tm, D), lambda i, j: (i, 0)),
                  pl.BlockSpec((1, D), lambda i, j: (0, 0)),
                  pl.BlockSpec((D, tn), lambda i, j: (0, j)),
                  pl.BlockSpec((tm, lw), lambda i, j: (i, 0)),
                  pl.BlockSpec((tm, lw), lambda i, j: (i, 0)),
                  pl.BlockSpec((tm, lw), lambda i, j: (i, 0))],
        out_specs=pl.BlockSpec((tm, tn), lambda i, j: (i, j)),
        scratch_shapes=[pltpu.VMEM((tm, D), BF16)],
        compiler_params=_cparams(2),
        name="qkv_rotary",
    )(h, g, w_bf, cos_t, sin_a, sin_b)


def _attn_kernel(q_ref, k_ref, v_ref, o_ref, m_ref, l_ref, acc_ref):
    grp = pl.program_id(2)
    L, lw = q_ref.shape
    T = ATT_BLOCK
    E = ATT_HEAD_DIM

    @pl.when(grp == 0)
    def _():
        m_ref[...] = jnp.full(m_ref.shape, NEG_BIG, F32)
        l_ref[...] = jnp.zeros_like(l_ref)
        acc_ref[...] = jnp.zeros_like(acc_ref)

    lane = lax.broadcasted_iota(jnp.int32, (T, lw), 1)
    head_lo = lane < E
    qi = lax.broadcasted_iota(jnp.int32, (T, T), 0)
    kj = lax.broadcasted_iota(jnp.int32, (T, T), 1)
    cur_ok = kj <= qi
    prev_ok = kj >= qi

    def run_group(dil):
        nblk = L // dil // T

        def body(it, carry):
            r = it // nblk
            n = it % nblk
            rows = pl.ds(r + dil * T * n, T, stride=dil)
            q = q_ref[rows, :] * (E ** -0.5)
            kc = k_ref[rows, :].astype(BF16)
            vc = v_ref[rows, :]
            if nblk > 1:
                prow = pl.ds(r + dil * T * jnp.maximum(n - 1, 0), T, stride=dil)
                kp = k_ref[prow, :].astype(BF16)
                vp = v_ref[prow, :]
                p_ok = prev_ok & (n > 0)
            m_old = m_ref[rows, :]
            l_old = l_ref[rows, :]
            m_new = []
            l_add = []
            pv = jnp.zeros((T, lw), F32)
            for hh in range(lw // E):
                hm = (lane >= hh * E) & (lane < (hh + 1) * E)
                qh = jnp.where(hm, q, 0.0).astype(BF16)
                s_c = jnp.where(cur_ok, _dot_nt(qh, kc), NEG_BIG)
                mh = jnp.max(s_c, axis=-1, keepdims=True)
                if nblk > 1:
                    s_p = jnp.where(p_ok, _dot_nt(qh, kp), NEG_BIG)
                    mh = jnp.maximum(mh, jnp.max(s_p, axis=-1, keepdims=True))
                mh = jnp.maximum(mh, m_old[:, hh * E:hh * E + 1])
                p_c = jnp.exp(s_c - mh)
                lh = jnp.sum(p_c, axis=-1, keepdims=True)
                pv = pv + _dot(p_c.astype(BF16), jnp.where(hm, vc, 0.0).astype(BF16))
                if nblk > 1:
                    p_p = jnp.exp(s_p - mh)
                    lh = lh + jnp.sum(p_p, axis=-1, keepdims=True)
                    pv = pv + _dot(p_p.astype(BF16), jnp.where(hm, vp, 0.0).astype(BF16))
                m_new.append(mh)
                l_add.append(lh)
            m_t = jnp.where(head_lo, m_new[0], m_new[1])
            alpha = jnp.exp(m_old - m_t)
            m_ref[rows, :] = m_t
            l_ref[rows, :] = alpha * l_old + jnp.where(head_lo, l_add[0], l_add[1])
            acc_ref[rows, :] = alpha * acc_ref[rows, :] + pv
            return carry

        lax.fori_loop(0, dil * nblk, body, 0)

    for gi, dil in enumerate(ATT_DILATIONS):
        pl.when(grp == gi)(functools.partial(run_group, dil))

    @pl.when(grp == len(ATT_DILATIONS) - 1)
    def _():
        o_ref[...] = acc_ref[...] / l_ref[...]


def _attn(qkv, B, L, gw):
    N = qkv.shape[0]
    lw = 2 * ATT_HEAD_DIM
    npair = gw // lw
    ng = len(ATT_DILATIONS)

    def spec(kind):
        return pl.BlockSpec((L, lw), lambda b, p, g: (b, kind * ng * npair + g * npair + p))

    return pl.pallas_call(
        _attn_kernel,
        out_shape=jax.ShapeDtypeStruct((N, gw), F32),
        grid=(B, npair, ng),
        in_specs=[spec(0), spec(1), spec(2)],
        out_specs=pl.BlockSpec((L, lw), lambda b, p, g: (b, p)),
        scratch_shapes=[pltpu.VMEM((L, lw), F32), pltpu.VMEM((L, lw), F32), pltpu.VMEM((L, lw), F32)],
        compiler_params=_cparams(3),
        name="dilated_attention",
    )(qkv, qkv, qkv)


def _s5_params(A_re, A_im, log_dt, B_re, B_im, C_re, C_im, Dd):
    G, P = A_re.shape
    dt = jnp.exp(log_dt)[:, None]
    mag = jnp.exp(A_re * dt)
    ab_re = mag * jnp.cos(A_im * dt)
    ab_im = mag * jnp.sin(A_im * dt)
    den = A_re * A_re + A_im * A_im
    nr, ni = ab_re - 1.0, ab_im
    c_re = (nr * A_re + ni * A_im) / den
    c_im = (ni * A_re - nr * A_im) / den
    Bb_re = c_re[..., None] * B_re - c_im[..., None] * B_im
    Bb_im = c_re[..., None] * B_im + c_im[..., None] * B_re
    hg = G // 2
    eye = jnp.eye(hg, dtype=F32)

    def bd_in(m):
        return jnp.einsum('gpc,gh->gchp', m, eye).reshape(hg * S5_GROUP, hg * P)

    def bd_out(m):
        return jnp.einsum('gcp,gh->gphc', m, eye).reshape(hg * P, hg * S5_GROUP)

    bds, cds = [], []
    for k in range(2):
        sl = slice(k * hg, (k + 1) * hg)
        bds.append(jnp.concatenate([bd_in(Bb_re[sl]), bd_in(Bb_im[sl])], axis=1).astype(BF16))
        cds.append(jnp.concatenate([bd_out(C_re[sl]), -bd_out(C_im[sl])], axis=0).astype(BF16))
    return bds[0], bds[1], ab_re.reshape(1, G * P), ab_im.reshape(1, G * P), cds[0], cds[1], Dd.reshape(1, -1)


def _rope_tables(positions, lw):
    half = ROT_DIM // 2
    inv_freq = ROPE_THETA ** (-jnp.arange(half, dtype=F32) * 2.0 / ROT_DIM)
    ang = positions.astype(F32).reshape(-1, 1) * inv_freq
    cos, sin = jnp.cos(ang), jnp.sin(ang)
    n = ang.shape[0]
    pad = ATT_HEAD_DIM - ROT_DIM
    c_head = jnp.concatenate([cos, cos, jnp.ones((n, pad), F32)], axis=1)
    a_head = jnp.concatenate([-sin, jnp.zeros((n, half + pad), F32)], axis=1)
    b_head = jnp.concatenate([jnp.zeros((n, half), F32), sin, jnp.zeros((n, pad), F32)], axis=1)
    rep = lw // ATT_HEAD_DIM
    return jnp.tile(c_head, (1, rep)), jnp.tile(a_head, (1, rep)), jnp.tile(b_head, (1, rep))


def kernel(x, positions, norm_mix, norm_ffn, norm_final, mix_w_in, mix_w_out, s5_A_re, s5_A_im, s5_log_dt, s5_B_re, s5_B_im, s5_C_re, s5_C_im, s5_D, s5_glu_w, s5_glu_b, hgrn_gamma, hgrn_norm, att_w_qkv, att_w_o, ffn_w_in, ffn_conv_w, ffn_conv_b, ffn_w_out):
    B, L, D = x.shape
    N = B * L
    x2 = x.reshape(N, D)
    tm = 512
    tps = L // tm
    s5w = s5_A_re.shape[1] * S5_GROUP
    hgw = hgrn_norm.shape[1]

    u_tm, hg = _inproj(x2, norm_mix[0:1], mix_w_in[0].astype(BF16), B, L, tm=tm, tn=s5w)
    s5p = _s5_params(s5_A_re[0], s5_A_im[0], s5_log_dt[0], s5_B_re[0], s5_B_im[0],
                     s5_C_re[0], s5_C_im[0], s5_D[0])
    oa_tm = _s5(u_tm.reshape(L * B, s5w), *s5p, s5_glu_w[0].astype(BF16), s5_glu_b[0:1], B)
    lb_all = jnp.cumsum(jax.nn.softmax(hgrn_gamma.astype(F32), axis=0), axis=0)
    ob = _hgrn(hg, lb_all[0:1], hgrn_norm[0:1], B, L)
    h = _proj_resid(
        [oa_tm.reshape(L, B * s5w), ob],
        [pl.BlockSpec((tm, s5w), lambda i: (i % tps, i // tps)),
         pl.BlockSpec((tm, hgw), lambda i: (i, 0))],
        mix_w_out[0].astype(BF16), x2, tm)
    h = _ffn(h, norm_ffn[0:1], ffn_w_in[0].astype(BF16), ffn_conv_w[0], ffn_conv_b[0:1],
             ffn_w_out[0].astype(BF16), norm_final.reshape(1, D), L, final_norm=False)

    gw = att_w_o.shape[1]
    lw = 2 * ATT_HEAD_DIM
    cos_t, sin_a, sin_b = _rope_tables(positions, lw)
    n_rot_cols = 2 * len(ATT_DILATIONS) * gw
    qkv = _qkv(h, norm_mix[1:2], att_w_qkv[0].astype(BF16), cos_t, sin_a, sin_b, n_rot_cols, tm=tm, tn=gw)
    ao = _attn(qkv, B, L, gw)
    h = _proj_resid([ao], [pl.BlockSpec((tm, gw), lambda i: (i, 0))], att_w_o[0].astype(BF16), h, tm)
    h = _ffn(h, norm_ffn[1:2], ffn_w_in[1].astype(BF16), ffn_conv_w[1], ffn_conv_b[1:2],
             ffn_w_out[1].astype(BF16), norm_final.reshape(1, D), L, final_norm=True)
    return h.reshape(B, L, D)
```

```python
import functools
import math

import numpy as np
import jax
import jax.numpy as jnp
from jax import lax
from jax.experimental import pallas as pl
from jax.experimental.pallas import tpu as pltpu

F32 = jnp.float32
BF16 = jnp.bfloat16

NORM_EPS = 1e-6
S5_GROUP = 16
S5_STATE = 64
HG_HEADS = 4
HG_BLOCK = 16
ATT_HEAD_DIM = 64
ATT_BLOCK = 128
ATT_DILATIONS = (1, 4, 16)
ROT_DIM = 16
ROPE_THETA = 500000.0
CONV_W = 3
NEG_BIG = -1e30
LOG2E = math.log2(math.e)
LANES = 128

VMEM_LIMIT = 56 * 1024 * 1024


def _cparams(n_axes):
    return pltpu.CompilerParams(dimension_semantics=("arbitrary",) * n_axes,
                                vmem_limit_bytes=VMEM_LIMIT)


def _resident(shape):
    return pl.BlockSpec(shape, lambda *_: (0,) * len(shape), pipeline_mode=pl.Buffered(1))


def _rms(x, g):
    return x * lax.rsqrt(jnp.mean(x * x, axis=-1, keepdims=True) + NORM_EPS) * g


def _sigmoid(x):
    return 1.0 / (1.0 + jnp.exp(-x))


def _dot(a, b):
    return jnp.dot(a, b, preferred_element_type=F32)


def _dot_nt(a, b):
    return lax.dot_general(a, b, (((1,), (1,)), ((), ())), preferred_element_type=F32)


def _dot_tn(a, b):
    return lax.dot_general(a, b, (((0,), (0,)), ((), ())), preferred_element_type=F32)


def _norm_proj_kernel(x_ref, g_ref, w_ref, *rest, tn, n_rot_tiles):
    if n_rot_tiles:
        cos_ref, sin_ref, o_ref, xn_ref = rest
    else:
        o_ref, xn_ref = rest
    xn_ref[...] = _rms(x_ref[...], g_ref[...]).astype(BF16)
    for j in range(w_ref.shape[1] // tn):
        y = _dot(xn_ref[...], w_ref[:, j * tn:(j + 1) * tn])
        if j < n_rot_tiles:
            c = cos_ref[...]
            s = sin_ref[...]
            for k in range(tn // LANES):
                t = y[:, k * LANES:(k + 1) * LANES]
                o_ref[:, j * tn + k * LANES:j * tn + (k + 1) * LANES] = (
                    t * c + pltpu.roll(t, LANES // 2, axis=1) * s)
        else:
            o_ref[:, j * tn:(j + 1) * tn] = y


def _norm_proj(x2, g, w_bf, tm, tn, rot=None, n_rot_cols=0, name="norm_proj"):
    N, D = x2.shape
    ncols = w_bf.shape[1]
    in_specs = [pl.BlockSpec((tm, D), lambda i: (i, 0)), _resident((1, D)), _resident(w_bf.shape)]
    args = [x2, g, w_bf]
    if rot is not None:
        in_specs += [pl.BlockSpec((tm, LANES), lambda i: (i, 0))] * 2
        args += list(rot)
    return pl.pallas_call(
        functools.partial(_norm_proj_kernel, tn=tn, n_rot_tiles=n_rot_cols // tn),
        out_shape=jax.ShapeDtypeStruct((N, ncols), F32),
        grid=(N // tm,),
        in_specs=in_specs,
        out_specs=pl.BlockSpec((tm, ncols), lambda i: (i, 0)),
        scratch_shapes=[pltpu.VMEM((tm, D), BF16)],
        compiler_params=_cparams(1),
        name=name,
    )(*args)


def _s5_kernel(u_ref, bd0_ref, bd1_ref, ar_ref, ai_ref, c0_ref, c1_ref, d_ref, gw_ref, gb_ref,
               o_ref, x0_ref, x1_ref, st_ref, pf_ref, pb_ref, *, lane_chunk):
    nb, ts, W = u_ref.shape
    tm = nb * ts
    half = x0_ref.shape[1] // 2

    @pl.when(pl.program_id(0) == 0)
    def _():
        st_ref[...] = jnp.zeros_like(st_ref)
        r = lax.broadcasted_iota(jnp.int32, (tm, tm), 0)
        c = lax.broadcasted_iota(jnp.int32, (tm, tm), 1)
        pf_ref[...] = jnp.where(c == (r % nb) * ts + r // nb, 1.0, 0.0).astype(BF16)
        pb_ref[...] = jnp.where(r == (c % nb) * ts + c // nb, 1.0, 0.0).astype(BF16)

    u_bm = u_ref[...].reshape(tm, W)
    u_hi = u_bm.astype(BF16)
    u_lo = (u_bm - u_hi.astype(F32)).astype(BF16)
    ut_hi = _dot(pf_ref[...], u_hi)
    u = ut_hi + _dot(pf_ref[...], u_lo)
    ub = ut_hi.astype(BF16)
    ku = W // 2
    x0_ref[...] = _dot(ub[:, :ku], bd0_ref[...])
    x1_ref[...] = _dot(ub[:, ku:], bd1_ref[...])

    for k, x_ref in enumerate((x0_ref, x1_ref)):
        for c in range(half // lane_chunk):
            lo = c * lane_chunk
            re = slice(lo, lo + lane_chunk)
            im = slice(half + lo, half + lo + lane_chunk)
            a_re = jnp.broadcast_to(ar_ref[:, k * half + lo:k * half + lo + lane_chunk], (nb, lane_chunk))
            a_im = jnp.broadcast_to(ai_ref[:, k * half + lo:k * half + lo + lane_chunk], (nb, lane_chunk))

            def body(s, carry, x_ref=x_ref, re=re, im=im, a_re=a_re, a_im=a_im):
                xr, xi = carry
                rows = pl.ds(pl.multiple_of(s * nb, nb), nb)
                nr = a_re * xr - a_im * xi + x_ref[rows, re]
                ni = a_re * xi + a_im * xr + x_ref[rows, im]
                x_ref[rows, re] = nr
                x_ref[rows, im] = ni
                return nr, ni

            xr, xi = lax.fori_loop(0, ts, body, (st_ref[2 * k, :, re], st_ref[2 * k + 1, :, re]),
                                   unroll=8)
            st_ref[2 * k, :, re] = xr
            st_ref[2 * k + 1, :, re] = xi

    y0 = _dot(x0_ref[...].astype(BF16), c0_ref[...])
    y1 = _dot(x1_ref[...].astype(BF16), c1_ref[...])
    y = jnp.concatenate([y0, y1], axis=1) + d_ref[...] * u
    z = 0.5 * y * (1.0 + jnp.tanh(math.sqrt(2.0 / math.pi) * (y + 0.044715 * (y * y * y))))
    gate = _sigmoid(_dot(z.astype(BF16), gw_ref[...]) + gb_ref[...])
    o_tm = (z * gate).astype(BF16)
    o_ref[...] = _dot(pb_ref[...], o_tm).astype(BF16).reshape(nb, ts, W)


def _s5(proj3, bd0, bd1, ar, ai, c0, c1, dd, gw, gb, W, ts=64):
    B, L, _ = proj3.shape
    S2 = bd0.shape[1]
    tm = B * ts
    kern = functools.partial(_s5_kernel, lane_chunk=512)
    return pl.pallas_call(
        kern,
        out_shape=jax.ShapeDtypeStruct((B, L, W), BF16),
        grid=(L // ts,),
        in_specs=[pl.BlockSpec((B, ts, W), lambda t: (0, t, 0))]
                 + [_resident(a.shape) for a in (bd0, bd1, ar, ai, c0, c1, dd, gw, gb)],
        out_specs=pl.BlockSpec((B, ts, W), lambda t: (0, t, 0)),
        scratch_shapes=[pltpu.VMEM((tm, S2), F32), pltpu.VMEM((tm, S2), F32),
                        pltpu.VMEM((4, B, S2 // 2), F32),
                        pltpu.VMEM((tm, tm), BF16), pltpu.VMEM((tm, tm), BF16)],
        compiler_params=_cparams(1),
        name="s5_mixer",
    )(proj3, bd0, bd1, ar, ai, c0, c1, dd, gw, gb)


def _hgrn_kernel(xq_ref, xf_ref, xi_ref, xg_ref, lb_ref, ng_ref, o_ref,
                 st_ref, qe_ref, ke_ref, eb_ref, q_ref, k_ref, b_ref, v_ref, oi_ref):
    tl, W = xq_ref.shape
    C = HG_BLOCK
    hd = W // HG_HEADS
    nblk = tl // C

    @pl.when(pl.program_id(1) == 0)
    def _():
        st_ref[...] = jnp.zeros_like(st_ref)
        k_ref[0:C, :] = jnp.zeros((C, W), F32)
        b_ref[0:C, :] = jnp.zeros((C, W), F32)
        v_ref[0:C, :] = jnp.zeros((C, W), F32)

    xq = xq_ref[...]
    q = xq * _sigmoid(xq)
    lb = lb_ref[...]
    f = lb + (1.0 - lb) * _sigmoid(xf_ref[...])
    kk = 1.0 - f
    g2 = jnp.log(f) * LOG2E
    v = xi_ref[...]

    row = lax.broadcasted_iota(jnp.int32, (tl, tl), 0)
    col = lax.broadcasted_iota(jnp.int32, (tl, tl), 1)
    same = (row // C) == (col // C)
    t_cum = jnp.where(same & (col <= row), 1.0, 0.0).astype(BF16)
    t_blk = jnp.where(same, 1.0, 0.0).astype(BF16)
    g_hi = g2.astype(BF16)
    g_lo = (g2 - g_hi.astype(F32)).astype(BF16)
    b = _dot(t_cum, g_hi) + _dot(t_cum, g_lo)
    bl = _dot(t_blk, g_hi) + _dot(t_blk, g_lo)

    q_ref[...] = q
    qe_ref[...] = (q * jnp.exp2(b)).astype(BF16)
    ke_ref[...] = (kk * jnp.exp2(bl - b)).astype(BF16)
    eb_ref[...] = jnp.exp2(bl)
    k_ref[C:, :] = kk
    b_ref[C:, :] = b
    v_ref[C:, :] = v

    for h in range(HG_HEADS):
        hl = slice(h * hd, (h + 1) * hd)
        st = st_ref[h]
        for i in range(nblk):
            rows = slice(i * C, (i + 1) * C)
            oi_ref[rows, hl] = _dot_nt(qe_ref[rows, hl], st.astype(BF16))
            vb = v_ref[C + i * C:C + (i + 1) * C, hl].astype(BF16)
            st = st * eb_ref[i * C:i * C + 1, hl] + _dot_tn(vb, ke_ref[rows, hl])
        st_ref[h] = st

    rc = 64 if tl % 64 == 0 else tl
    lag_ok = [(lax.broadcasted_iota(jnp.int32, (rc, 1), 0) % C) >= d for d in range(C)]
    for h in range(HG_HEADS):
        hl = slice(h * hd, (h + 1) * hd)
        for r0 in range(0, tl, rc):
            qc = q_ref[r0:r0 + rc, hl]
            bc = b_ref[C + r0:C + r0 + rc, hl]
            acc = oi_ref[r0:r0 + rc, hl]
            for d in range(C):
                s0 = C + r0 - d
                e = qc * k_ref[s0:s0 + rc, hl] * jnp.exp2(bc - b_ref[s0:s0 + rc, hl])
                a = jnp.where(lag_ok[d], jnp.sum(e, axis=-1, keepdims=True), 0.0)
                acc = acc + a * v_ref[s0:s0 + rc, hl]
            on = acc * lax.rsqrt(jnp.mean(acc * acc, axis=-1, keepdims=True) + NORM_EPS)
            xg = xg_ref[r0:r0 + rc, hl]
            o_ref[r0:r0 + rc, hl] = (on * ng_ref[:, hl] * (xg * _sigmoid(xg))).astype(o_ref.dtype)


def _hgrn(proj, lb, ng, B, L, col0, tl=128):
    N = proj.shape[0]
    W = lb.shape[1]
    hd = W // HG_HEADS
    tps = L // tl
    C = HG_BLOCK
    col = lambda c: pl.BlockSpec((tl, W), lambda b, t: (b * tps + t, col0 + c))
    return pl.pallas_call(
        _hgrn_kernel,
        out_shape=jax.ShapeDtypeStruct((N, W), BF16),
        grid=(B, tps),
        in_specs=[col(0), col(1), col(2), col(3), _resident((1, W)), _resident((1, W))],
        out_specs=pl.BlockSpec((tl, W), lambda b, t: (b * tps + t, 0)),
        scratch_shapes=[pltpu.VMEM((HG_HEADS, hd, hd), F32),
                        pltpu.VMEM((tl, W), BF16),
                        pltpu.VMEM((tl, W), BF16),
                        pltpu.VMEM((tl, W), F32),
                        pltpu.VMEM((tl, W), F32),
                        pltpu.VMEM((tl + C, W), F32),
                        pltpu.VMEM((tl + C, W), F32),
                        pltpu.VMEM((tl + C, W), F32),
                        pltpu.VMEM((tl, W), F32)],
        compiler_params=_cparams(2),
        name="hgrn2_mixer",
    )(proj, proj, proj, proj, lb, ng)


def _proj_resid_kernel(*refs, n_in):
    a_refs = refs[:n_in]
    w_ref, r_ref, o_ref = refs[n_in:]
    acc = r_ref[...]
    k0 = 0
    for a_ref in a_refs:
        kw = a_ref.shape[1]
        acc = acc + _dot(a_ref[...].astype(BF16), w_ref[k0:k0 + kw, :])
        k0 += kw
    o_ref[...] = acc


def _proj_resid(a_list, w_bf, resid, tm):
    N, D = resid.shape
    return pl.pallas_call(
        functools.partial(_proj_resid_kernel, n_in=len(a_list)),
        out_shape=jax.ShapeDtypeStruct((N, D), F32),
        grid=(N // tm,),
        in_specs=[pl.BlockSpec((tm, a.shape[1]), lambda i: (i, 0)) for a in a_list]
                 + [_resident(w_bf.shape), pl.BlockSpec((tm, D), lambda i: (i, 0))],
        out_specs=pl.BlockSpec((tm, D), lambda i: (i, 0)),
        compiler_params=_cparams(1),
        name="proj_resid",
    )(*a_list, w_bf, resid)


FFN_HALO = 16


def _ffn_kernel(x_ref, halo_ref, g_ref, win_ref, cw_ref, cb_ref, wo_ref, gf_ref, o_ref,
                xn_ref, act_ref, a_ref, b_ref, *, tiles_per_seq, final_norm, tf):
    i = pl.program_id(0)
    tm = x_ref.shape[0]
    H = FFN_HALO
    dff = wo_ref.shape[0]

    xn_ref[H:, :] = _rms(x_ref[...], g_ref[...]).astype(BF16)
    hn = _rms(halo_ref[...], g_ref[...])
    xn_ref[0:H, :] = jnp.where(i % tiles_per_seq == 0, 0.0, hn).astype(BF16)

    def conv(s_ref, c0):
        cols = slice(c0, c0 + tf)
        out = cb_ref[:, cols] + s_ref[H:H + tm, :] * cw_ref[CONV_W - 1:CONV_W, cols]
        for t in range(1, CONV_W):
            out = out + s_ref[H - t:H - t + tm, :] * cw_ref[CONV_W - 1 - t:CONV_W - t, cols]
        return out

    for j in range(dff // tf):
        sa = a_ref.at[j % 2]
        sb = b_ref.at[j % 2]
        sa[...] = _dot(xn_ref[...], win_ref[:, j * tf:(j + 1) * tf])
        sb[...] = _dot(xn_ref[...], win_ref[:, dff + j * tf:dff + (j + 1) * tf])
        a = conv(sa, j * tf)
        b = conv(sb, dff + j * tf)
        act_ref[:, j * tf:(j + 1) * tf] = (a * _sigmoid(a) * b).astype(BF16)

    y = x_ref[...] + _dot(act_ref[...], wo_ref[...])
    if final_norm:
        y = _rms(y, gf_ref[...])
    o_ref[...] = y


def _ffn(h, g, w_in_bf, conv_w, conv_b, w_out_bf, g_final, L, final_norm, tm=512, tf=256):
    N, D = h.shape
    dff = w_out_bf.shape[0]
    H = FFN_HALO
    kern = functools.partial(_ffn_kernel, tiles_per_seq=L // tm, final_norm=final_norm, tf=tf)
    return pl.pallas_call(
        kern,
        out_shape=jax.ShapeDtypeStruct((N, D), F32),
        grid=(N // tm,),
        in_specs=[pl.BlockSpec((tm, D), lambda i: (i, 0)),
                  pl.BlockSpec((H, D), lambda i: (jnp.maximum(i * (tm // H) - 1, 0), 0)),
                  _resident((1, D)), _resident(w_in_bf.shape), _resident(conv_w.shape),
                  _resident(conv_b.shape), _resident(w_out_bf.shape), _resident((1, D))],
        out_specs=pl.BlockSpec((tm, D), lambda i: (i, 0)),
        scratch_shapes=[pltpu.VMEM((tm + H, D), BF16),
                        pltpu.VMEM((tm, dff), BF16),
                        pltpu.VMEM((2, tm + H, tf), F32),
                        pltpu.VMEM((2, tm + H, tf), F32)],
        compiler_params=_cparams(1),
        name="conv_ffn",
    )(h, h, g, w_in_bf, conv_w, conv_b, w_out_bf, g_final)


def _rot_lane_layout():
    half = ROT_DIM // 2
    E = ATT_HEAD_DIM
    lay = []
    for base in (0, half):
        lay += [(0, base + i) for i in range(half)] + [(1, base + i) for i in range(half)]
        lay += [(base // half, d) for d in range(ROT_DIM, E)]
    return lay


def _attn_kernel(q_ref, k_ref, v_ref, o_ref, m_ref, acc_a_ref, acc_b_ref,
                 qs_ref, ka_ref, kb_ref, va_ref, vb_ref):
    grp = pl.program_id(2)
    L, lw = q_ref.shape
    T = ATT_BLOCK
    E = ATT_HEAD_DIM
    n_groups = len(ATT_DILATIONS)

    lane1 = lax.broadcasted_iota(jnp.int32, (1, lw), 1)
    head_of_lane = np.array([h for h, _ in _rot_lane_layout()])
    in_a_qk = jnp.zeros((1, lw), jnp.bool_)
    for lo, hi in ((0, 8), (16, 72)):
        in_a_qk = in_a_qk | ((lane1 >= lo) & (lane1 < hi))
    assert all((head_of_lane[l] == 0) == (l < 8 or 16 <= l < 72) for l in range(lw))
    in_a_v = lane1 < E
    qi = lax.broadcasted_iota(jnp.int32, (T, T), 0)
    kj = lax.broadcasted_iota(jnp.int32, (T, T), 1)
    cur_ok = kj <= qi
    both_ok = jnp.concatenate([kj >= qi, cur_ok], axis=1)

    def run_group(gi, dil):
        M = L // dil
        nblk = M // T
        for r in range(dil):
            src = pl.ds(r, M, stride=dil) if dil > 1 else slice(0, M)
            dst = slice(r * M, (r + 1) * M)
            qs_ref[dst, :] = (q_ref[src, :] * (E ** -0.5 * LOG2E)).astype(BF16)
            kf = k_ref[src, :]
            ka_ref[dst, :] = jnp.where(in_a_qk, kf, 0.0).astype(BF16)
            kb_ref[dst, :] = jnp.where(in_a_qk, 0.0, kf).astype(BF16)
            vf = v_ref[src, :]
            va_ref[dst, :] = jnp.where(in_a_v, vf, 1.0).astype(BF16)
            vb_ref[dst, :] = jnp.where(in_a_v, 1.0, vf).astype(BF16)

        for r in range(dil):
            for n in range(nblk):
                c0 = r * M + n * T
                rows = pl.ds(r + dil * T * n, T, stride=dil) if dil > 1 else slice(n * T, (n + 1) * T)
                kv_rows = slice(c0 - T, c0 + T) if n > 0 else slice(c0, c0 + T)
                ok = both_ok if n > 0 else cur_ok
                q = qs_ref[c0:c0 + T, :]
                s_a = jnp.where(ok, _dot_nt(q, ka_ref[kv_rows, :]), NEG_BIG)
                s_b = jnp.where(ok, _dot_nt(q, kb_ref[kv_rows, :]), NEG_BIG)
                m_a = jnp.max(s_a, axis=-1, keepdims=True)
                m_b = jnp.max(s_b, axis=-1, keepdims=True)
                if gi > 0:
                    m_old = m_ref[rows, :]
                    m_a = jnp.maximum(m_a, m_old[:, 0:1])
                    m_b = jnp.maximum(m_b, m_old[:, E:E + 1])
                p_a = jnp.exp2(s_a - m_a).astype(BF16)
                p_b = jnp.exp2(s_b - m_b).astype(BF16)
                pv_a = _dot(p_a, va_ref[kv_rows, :])
                pv_b = _dot(p_b, vb_ref[kv_rows, :])
                if gi > 0:
                    pv_a = jnp.exp2(m_old[:, 0:1] - m_a) * acc_a_ref[rows, :] + pv_a
                    pv_b = jnp.exp2(m_old[:, E:E + 1] - m_b) * acc_b_ref[rows, :] + pv_b
                if gi < n_groups - 1:
                    m_ref[rows, :] = jnp.where(in_a_v, m_a, m_b)
                    acc_a_ref[rows, :] = pv_a
                    acc_b_ref[rows, :] = pv_b
                else:
                    o_ref[rows, :] = jnp.where(in_a_v, pv_a / pltpu.roll(pv_a, E, axis=1),
                                               pv_b / pltpu.roll(pv_b, E, axis=1))

    for gi, dil in enumerate(ATT_DILATIONS):
        pl.when(grp == gi)(functools.partial(run_group, gi, dil))


def _attn(qkv, B, L, gw):
    N = qkv.shape[0]
    lw = 2 * ATT_HEAD_DIM
    npair = gw // lw
    ng = len(ATT_DILATIONS)

    def spec(kind):
        return pl.BlockSpec((L, lw), lambda b, p, g: (b, kind * ng * npair + g * npair + p))

    return pl.pallas_call(
        _attn_kernel,
        out_shape=jax.ShapeDtypeStruct((N, gw), F32),
        grid=(B, npair, ng),
        in_specs=[spec(0), spec(1), spec(2)],
        out_specs=pl.BlockSpec((L, lw), lambda b, p, g: (b, p)),
        scratch_shapes=[pltpu.VMEM((L, lw), F32)] * 3 + [pltpu.VMEM((L, lw), BF16)] * 5,
        compiler_params=_cparams(3),
        name="dilated_attention",
    )(qkv, qkv, qkv)


def _s5_params(A_re, A_im, log_dt, B_re, B_im, C_re, C_im, Dd):
    G, P = A_re.shape
    dt = jnp.exp(log_dt)[:, None]
    mag = jnp.exp(A_re * dt)
    ab_re = mag * jnp.cos(A_im * dt)
    ab_im = mag * jnp.sin(A_im * dt)
    den = A_re * A_re + A_im * A_im
    nr, ni = ab_re - 1.0, ab_im
    c_re = (nr * A_re + ni * A_im) / den
    c_im = (ni * A_re - nr * A_im) / den
    Bb_re = c_re[..., None] * B_re - c_im[..., None] * B_im
    Bb_im = c_re[..., None] * B_im + c_im[..., None] * B_re
    hg = G // 2
    eye = jnp.eye(hg, dtype=F32)

    def bd_in(m):
        return jnp.einsum('gpc,gh->gchp', m, eye).reshape(hg * S5_GROUP, hg * P)

    def bd_out(m):
        return jnp.einsum('gcp,gh->gphc', m, eye).reshape(hg * P, hg * S5_GROUP)

    bds, cds = [], []
    for k in range(2):
        sl = slice(k * hg, (k + 1) * hg)
        bds.append(jnp.concatenate([bd_in(Bb_re[sl]), bd_in(Bb_im[sl])], axis=1).astype(BF16))
        cds.append(jnp.concatenate([bd_out(C_re[sl]), -bd_out(C_im[sl])], axis=0).astype(BF16))
    return bds[0], bds[1], ab_re.reshape(1, G * P), ab_im.reshape(1, G * P), cds[0], cds[1], Dd.reshape(1, -1)


def _rope_tables(positions):
    half = ROT_DIM // 2
    inv_freq = ROPE_THETA ** (-jnp.arange(half, dtype=F32) * 2.0 / ROT_DIM)
    ang = positions.astype(F32).reshape(-1, 1) * inv_freq
    cos, sin = jnp.cos(ang), jnp.sin(ang)
    n = ang.shape[0]
    pad = LANES // 2 - ROT_DIM
    c_t = jnp.concatenate([cos, cos, jnp.ones((n, pad), F32)] * 2, axis=1)
    s_t = jnp.concatenate([-sin, -sin, jnp.zeros((n, pad), F32), sin, sin, jnp.zeros((n, pad), F32)], axis=1)
    return c_t, s_t


def _qk_column_perm(n_cols):
    lay = _rot_lane_layout()
    E = ATT_HEAD_DIM
    perm = []
    for slab in range(n_cols // LANES):
        perm += [slab * LANES + h * E + d for h, d in lay]
    return np.array(perm, dtype=np.int32)


def kernel(x, positions, norm_mix, norm_ffn, norm_final, mix_w_in, mix_w_out, s5_A_re, s5_A_im, s5_log_dt, s5_B_re, s5_B_im, s5_C_re, s5_C_im, s5_D, s5_glu_w, s5_glu_b, hgrn_gamma, hgrn_norm, att_w_qkv, att_w_o, ffn_w_in, ffn_conv_w, ffn_conv_b, ffn_w_out):
    B, L, D = x.shape
    N = B * L
    x2 = x.reshape(N, D)
    tm = 512
    s5w = s5_A_re.shape[1] * S5_GROUP
    hgw = hgrn_norm.shape[1]

    proj = _norm_proj(x2, norm_mix[0:1], mix_w_in[0].astype(BF16), tm, s5w, name="inproj")
    s5p = _s5_params(s5_A_re[0], s5_A_im[0], s5_log_dt[0], s5_B_re[0], s5_B_im[0],
                     s5_C_re[0], s5_C_im[0], s5_D[0])
    oa = _s5(proj.reshape(B, L, -1), *s5p, s5_glu_w[0].astype(BF16), s5_glu_b[0:1], s5w)
    lb_all = jnp.cumsum(jax.nn.softmax(hgrn_gamma.astype(F32), axis=0), axis=0)
    ob = _hgrn(proj, lb_all[0:1], hgrn_norm[0:1], B, L, col0=s5w // hgw)
    h = _proj_resid([oa.reshape(N, s5w), ob], mix_w_out[0].astype(BF16), x2, tm)
    h = _ffn(h, norm_ffn[0:1], ffn_w_in[0].astype(BF16), ffn_conv_w[0], ffn_conv_b[0:1],
             ffn_w_out[0].astype(BF16), norm_final.reshape(1, D), L, final_norm=False)

    gw = att_w_o.shape[1]
    n_rot_cols = 2 * len(ATT_DILATIONS) * gw
    w_qkv = att_w_qkv[0]
    w_qkv = jnp.concatenate([w_qkv[:, _qk_column_perm(n_rot_cols)], w_qkv[:, n_rot_cols:]], axis=1)
    qkv = _norm_proj(h, norm_mix[1:2], w_qkv.astype(BF16), tm, gw, rot=_rope_tables(positions),
                     n_rot_cols=n_rot_cols, name="qkv_rotary")
    ao = _attn(qkv, B, L, gw)
    h = _proj_resid([ao], att_w_o[0].astype(BF16), h, tm)
    h = _ffn(h, norm_ffn[1:2], ffn_w_in[1].astype(BF16), ffn_conv_w[1], ffn_conv_b[1:2],
             ffn_w_out[1].astype(BF16), norm_final.reshape(1, D), L, final_norm=True)
    return h.reshape(B, L, D)
```

```python
import functools
import math

import numpy as np
import jax
import jax.numpy as jnp
from jax import lax
from jax.experimental import pallas as pl
from jax.experimental.pallas import tpu as pltpu

F32 = jnp.float32
BF16 = jnp.bfloat16

NORM_EPS = 1e-6
S5_GROUP = 16
S5_STATE = 64
HG_HEADS = 4
HG_BLOCK = 16
ATT_HEAD_DIM = 64
ATT_BLOCK = 128
ATT_DILATIONS = (1, 4, 16)
ROT_DIM = 16
ROPE_THETA = 500000.0
CONV_W = 3
NEG_BIG = -1e30
LOG2E = math.log2(math.e)
LANES = 128

VMEM_LIMIT = 56 * 1024 * 1024


def _cparams(n_axes):
    return pltpu.CompilerParams(dimension_semantics=("arbitrary",) * n_axes,
                                vmem_limit_bytes=VMEM_LIMIT)


def _resident(shape):
    return pl.BlockSpec(shape, lambda *_: (0,) * len(shape), pipeline_mode=pl.Buffered(1))


def _rms(x, g):
    return x * lax.rsqrt(jnp.mean(x * x, axis=-1, keepdims=True) + NORM_EPS) * g


def _sigmoid(x):
    return 1.0 / (1.0 + jnp.exp(-x))


def _dot(a, b):
    return jnp.dot(a, b, preferred_element_type=F32)


def _dot_nt(a, b):
    return lax.dot_general(a, b, (((1,), (1,)), ((), ())), preferred_element_type=F32)


def _dot_tn(a, b):
    return lax.dot_general(a, b, (((0,), (0,)), ((), ())), preferred_element_type=F32)


def _norm_proj_kernel(x_ref, g_ref, w_ref, *rest, tn, n_rot_tiles):
    if n_rot_tiles:
        cs_ref, ex_ref, o_ref, xn_ref = rest
        cs = cs_ref[...]
        cs_hi = cs.astype(BF16)
        cs_lo = (cs - cs_hi.astype(F32)).astype(BF16)
        tab = _dot(cs_hi, ex_ref[...]) + _dot(cs_lo, ex_ref[...])
        c = tab[:, :LANES]
        s = tab[:, LANES:]
    else:
        o_ref, xn_ref = rest
    xn_ref[...] = _rms(x_ref[...], g_ref[...]).astype(BF16)
    for j in range(w_ref.shape[1] // tn):
        y = _dot(xn_ref[...], w_ref[:, j * tn:(j + 1) * tn])
        if j < n_rot_tiles:
            for k in range(tn // LANES):
                t = y[:, k * LANES:(k + 1) * LANES]
                o_ref[:, j * tn + k * LANES:j * tn + (k + 1) * LANES] = (
                    t * c + pltpu.roll(t, LANES // 2, axis=1) * s)
        else:
            o_ref[:, j * tn:(j + 1) * tn] = y


def _norm_proj(x2, g, w_bf, tm, tn, rot=None, n_rot_cols=0, name="norm_proj"):
    N, D = x2.shape
    ncols = w_bf.shape[1]
    in_specs = [pl.BlockSpec((tm, D), lambda i: (i, 0)), _resident((1, D)), _resident(w_bf.shape)]
    args = [x2, g, w_bf]
    if rot is not None:
        cs, expand = rot
        in_specs += [pl.BlockSpec((tm, cs.shape[1]), lambda i: (i, 0)), _resident(expand.shape)]
        args += [cs, expand]
    return pl.pallas_call(
        functools.partial(_norm_proj_kernel, tn=tn, n_rot_tiles=n_rot_cols // tn),
        out_shape=jax.ShapeDtypeStruct((N, ncols), F32),
        grid=(N // tm,),
        in_specs=in_specs,
        out_specs=pl.BlockSpec((tm, ncols), lambda i: (i, 0)),
        scratch_shapes=[pltpu.VMEM((tm, D), BF16)],
        compiler_params=_cparams(1),
        name=name,
    )(*args)


def _s5_kernel(u_ref, bd0_ref, bd1_ref, ar_ref, ai_ref, c0_ref, c1_ref, d_ref, gw_ref, gb_ref,
               o_ref, x0_ref, x1_ref, st_ref, pf_ref, pb_ref, *, lane_chunk):
    nb, ts, W = u_ref.shape
    tm = nb * ts
    half = x0_ref.shape[1] // 2

    @pl.when(pl.program_id(0) == 0)
    def _():
        st_ref[...] = jnp.zeros_like(st_ref)
        r = lax.broadcasted_iota(jnp.int32, (tm, tm), 0)
        c = lax.broadcasted_iota(jnp.int32, (tm, tm), 1)
        pf_ref[...] = jnp.where(c == (r % nb) * ts + r // nb, 1.0, 0.0).astype(BF16)
        pb_ref[...] = jnp.where(r == (c % nb) * ts + c // nb, 1.0, 0.0).astype(BF16)

    u_bm = u_ref[...].reshape(tm, W)
    u_hi = u_bm.astype(BF16)
    u_lo = (u_bm - u_hi.astype(F32)).astype(BF16)
    ut_hi = _dot(pf_ref[...], u_hi)
    u = ut_hi + _dot(pf_ref[...], u_lo)
    ub = ut_hi.astype(BF16)
    ku = W // 2
    x0_ref[...] = _dot(ub[:, :ku], bd0_ref[...])
    x1_ref[...] = _dot(ub[:, ku:], bd1_ref[...])

    for k, x_ref in enumerate((x0_ref, x1_ref)):
        for c in range(half // lane_chunk):
            lo = c * lane_chunk
            re = slice(lo, lo + lane_chunk)
            im = slice(half + lo, half + lo + lane_chunk)
            a_re = jnp.broadcast_to(ar_ref[:, k * half + lo:k * half + lo + lane_chunk], (nb, lane_chunk))
            a_im = jnp.broadcast_to(ai_ref[:, k * half + lo:k * half + lo + lane_chunk], (nb, lane_chunk))

            def body(s, carry, x_ref=x_ref, re=re, im=im, a_re=a_re, a_im=a_im):
                xr, xi = carry
                rows = pl.ds(pl.multiple_of(s * nb, nb), nb)
                nr = a_re * xr - a_im * xi + x_ref[rows, re]
                ni = a_re * xi + a_im * xr + x_ref[rows, im]
                x_ref[rows, re] = nr
                x_ref[rows, im] = ni
                return nr, ni

            xr, xi = lax.fori_loop(0, ts, body, (st_ref[2 * k, :, re], st_ref[2 * k + 1, :, re]),
                                   unroll=8)
            st_ref[2 * k, :, re] = xr
            st_ref[2 * k + 1, :, re] = xi

    y0 = _dot(x0_ref[...].astype(BF16), c0_ref[...])
    y1 = _dot(x1_ref[...].astype(BF16), c1_ref[...])
    y = jnp.concatenate([y0, y1], axis=1) + d_ref[...] * u
    z = 0.5 * y * (1.0 + jnp.tanh(math.sqrt(2.0 / math.pi) * (y + 0.044715 * (y * y * y))))
    gate = _sigmoid(_dot(z.astype(BF16), gw_ref[...]) + gb_ref[...])
    o_tm = (z * gate).astype(BF16)
    o_ref[...] = _dot(pb_ref[...], o_tm).astype(BF16).reshape(nb, ts, W)


def _s5(proj3, bd0, bd1, ar, ai, c0, c1, dd, gw, gb, W, ts=64):
    B, L, _ = proj3.shape
    S2 = bd0.shape[1]
    tm = B * ts
    kern = functools.partial(_s5_kernel, lane_chunk=512)
    return pl.pallas_call(
        kern,
        out_shape=jax.ShapeDtypeStruct((B, L, W), BF16),
        grid=(L // ts,),
        in_specs=[pl.BlockSpec((B, ts, W), lambda t: (0, t, 0))]
                 + [_resident(a.shape) for a in (bd0, bd1, ar, ai, c0, c1, dd, gw, gb)],
        out_specs=pl.BlockSpec((B, ts, W), lambda t: (0, t, 0)),
        scratch_shapes=[pltpu.VMEM((tm, S2), F32), pltpu.VMEM((tm, S2), F32),
                        pltpu.VMEM((4, B, S2 // 2), F32),
                        pltpu.VMEM((tm, tm), BF16), pltpu.VMEM((tm, tm), BF16)],
        compiler_params=_cparams(1),
        name="s5_mixer",
    )(proj3, bd0, bd1, ar, ai, c0, c1, dd, gw, gb)


def _hgrn_kernel(xq_ref, xf_ref, xi_ref, xg_ref, lb_ref, ng_ref, o_ref,
                 st_ref, qe_ref, ke_ref, eb_ref, q_ref, k_ref, f_ref, oi_ref):
    tl, W = xq_ref.shape
    C = HG_BLOCK
    hd = W // HG_HEADS
    nblk = tl // C

    @pl.when(pl.program_id(1) == 0)
    def _():
        st_ref[...] = jnp.zeros_like(st_ref)

    xq = xq_ref[...]
    q = xq * _sigmoid(xq)
    lb = lb_ref[...]
    f = lb + (1.0 - lb) * _sigmoid(xf_ref[...])
    kk = 1.0 - f
    g2 = jnp.log(f) * LOG2E

    row = lax.broadcasted_iota(jnp.int32, (tl, tl), 0)
    col = lax.broadcasted_iota(jnp.int32, (tl, tl), 1)
    same = (row // C) == (col // C)
    t_cum = jnp.where(same & (col <= row), 1.0, 0.0).astype(BF16)
    t_blk = jnp.where(same, 1.0, 0.0).astype(BF16)
    g_hi = g2.astype(BF16)
    g_lo = (g2 - g_hi.astype(F32)).astype(BF16)
    b = _dot(t_cum, g_hi) + _dot(t_cum, g_lo)
    bl = _dot(t_blk, g_hi) + _dot(t_blk, g_lo)

    q_ref[...] = q
    k_ref[...] = kk
    f_ref[...] = f
    qe_ref[...] = (q * jnp.exp2(b)).astype(BF16)
    ke_ref[...] = (kk * jnp.exp2(bl - b)).astype(BF16)
    eb_ref[...] = jnp.exp2(bl)

    for h in range(HG_HEADS):
        hl = slice(h * hd, (h + 1) * hd)
        st = st_ref[h]
        for i in range(nblk):
            rows = slice(i * C, (i + 1) * C)
            oi_ref[rows, hl] = _dot_nt(qe_ref[rows, hl], st.astype(BF16))
            vb = xi_ref[rows, hl].astype(BF16)
            st = st * eb_ref[i * C:i * C + 1, hl] + _dot_tn(vb, ke_ref[rows, hl])
        st_ref[h] = st

    rc = 64 if tl % 64 == 0 else tl
    r_i = lax.broadcasted_iota(jnp.int32, (rc, tl), 0)
    c_i = lax.broadcasted_iota(jnp.int32, (rc, tl), 1)
    for r0 in range(0, tl, rc):
        lag = jnp.where(((r_i + r0) // C) == (c_i // C), r_i + r0 - c_i, -1)
        for h in range(HG_HEADS):
            hl = slice(h * hd, (h + 1) * hd)
            qc = q_ref[r0:r0 + rc, hl]
            fc = f_ref[r0:r0 + rc, hl]
            p = k_ref[r0:r0 + rc, hl]
            a_mat = jnp.zeros((rc, tl), F32)
            for d in range(C):
                if d > 0:
                    p = fc * pltpu.roll(p, 1, axis=0)
                a = jnp.sum(qc * p, axis=-1, keepdims=True)
                a_mat = jnp.where(lag == d, a, a_mat)
            acc = oi_ref[r0:r0 + rc, hl] + _dot(a_mat.astype(BF16), xi_ref[:, hl].astype(BF16))
            on = acc * lax.rsqrt(jnp.mean(acc * acc, axis=-1, keepdims=True) + NORM_EPS)
            xg = xg_ref[r0:r0 + rc, hl]
            o_ref[r0:r0 + rc, hl] = (on * ng_ref[:, hl] * (xg * _sigmoid(xg))).astype(o_ref.dtype)


def _hgrn(proj, lb, ng, B, L, col0, tl=128):
    N = proj.shape[0]
    W = lb.shape[1]
    hd = W // HG_HEADS
    tps = L // tl
    col = lambda c: pl.BlockSpec((tl, W), lambda b, t: (b * tps + t, col0 + c))
    return pl.pallas_call(
        _hgrn_kernel,
        out_shape=jax.ShapeDtypeStruct((N, W), BF16),
        grid=(B, tps),
        in_specs=[col(0), col(1), col(2), col(3), _resident((1, W)), _resident((1, W))],
        out_specs=pl.BlockSpec((tl, W), lambda b, t: (b * tps + t, 0)),
        scratch_shapes=[pltpu.VMEM((HG_HEADS, hd, hd), F32),
                        pltpu.VMEM((tl, W), BF16),
                        pltpu.VMEM((tl, W), BF16),
                        pltpu.VMEM((tl, W), F32),
                        pltpu.VMEM((tl, W), F32),
                        pltpu.VMEM((tl, W), F32),
                        pltpu.VMEM((tl, W), F32),
                        pltpu.VMEM((tl, W), F32)],
        compiler_params=_cparams(2),
        name="hgrn2_mixer",
    )(proj, proj, proj, proj, lb, ng)


def _proj_resid_kernel(*refs, n_in):
    a_refs = refs[:n_in]
    w_ref, r_ref, o_ref = refs[n_in:]
    acc = r_ref[...]
    k0 = 0
    for a_ref in a_refs:
        kw = a_ref.shape[1]
        acc = acc + _dot(a_ref[...].astype(BF16), w_ref[k0:k0 + kw, :])
        k0 += kw
    o_ref[...] = acc


def _proj_resid(a_list, w_bf, resid, tm):
    N, D = resid.shape
    return pl.pallas_call(
        functools.partial(_proj_resid_kernel, n_in=len(a_list)),
        out_shape=jax.ShapeDtypeStruct((N, D), F32),
        grid=(N // tm,),
        in_specs=[pl.BlockSpec((tm, a.shape[1]), lambda i: (i, 0)) for a in a_list]
                 + [_resident(w_bf.shape), pl.BlockSpec((tm, D), lambda i: (i, 0))],
        out_specs=pl.BlockSpec((tm, D), lambda i: (i, 0)),
        compiler_params=_cparams(1),
        name="proj_resid",
    )(*a_list, w_bf, resid)


FFN_HALO = 16


def _ffn_kernel(x_ref, halo_ref, g_ref, win_ref, cw_ref, cb_ref, wo_ref, gf_ref, o_ref,
                xn_ref, act_ref, a_ref, b_ref, *, tiles_per_seq, final_norm, tf):
    i = pl.program_id(0)
    tm = x_ref.shape[0]
    H = FFN_HALO
    dff = wo_ref.shape[0]

    xn_ref[H:, :] = _rms(x_ref[...], g_ref[...]).astype(BF16)
    hn = _rms(halo_ref[...], g_ref[...])
    xn_ref[0:H, :] = jnp.where(i % tiles_per_seq == 0, 0.0, hn).astype(BF16)

    def conv(s_ref, c0):
        cols = slice(c0, c0 + tf)
        out = cb_ref[:, cols] + s_ref[H:H + tm, :] * cw_ref[CONV_W - 1:CONV_W, cols]
        for t in range(1, CONV_W):
            out = out + s_ref[H - t:H - t + tm, :] * cw_ref[CONV_W - 1 - t:CONV_W - t, cols]
        return out

    for j in range(dff // tf):
        sa = a_ref.at[j % 2]
        sb = b_ref.at[j % 2]
        sa[...] = _dot(xn_ref[...], win_ref[:, j * tf:(j + 1) * tf])
        sb[...] = _dot(xn_ref[...], win_ref[:, dff + j * tf:dff + (j + 1) * tf])
        a = conv(sa, j * tf)
        b = conv(sb, dff + j * tf)
        act_ref[:, j * tf:(j + 1) * tf] = (a * _sigmoid(a) * b).astype(BF16)

    y = x_ref[...] + _dot(act_ref[...], wo_ref[...])
    if final_norm:
        y = _rms(y, gf_ref[...])
    o_ref[...] = y


def _ffn(h, g, w_in_bf, conv_w, conv_b, w_out_bf, g_final, L, final_norm, tm=512, tf=256):
    N, D = h.shape
    dff = w_out_bf.shape[0]
    H = FFN_HALO
    kern = functools.partial(_ffn_kernel, tiles_per_seq=L // tm, final_norm=final_norm, tf=tf)
    return pl.pallas_call(
        kern,
        out_shape=jax.ShapeDtypeStruct((N, D), F32),
        grid=(N // tm,),
        in_specs=[pl.BlockSpec((tm, D), lambda i: (i, 0)),
                  pl.BlockSpec((H, D), lambda i: (jnp.maximum(i * (tm // H) - 1, 0), 0)),
                  _resident((1, D)), _resident(w_in_bf.shape), _resident(conv_w.shape),
                  _resident(conv_b.shape), _resident(w_out_bf.shape), _resident((1, D))],
        out_specs=pl.BlockSpec((tm, D), lambda i: (i, 0)),
        scratch_shapes=[pltpu.VMEM((tm + H, D), BF16),
                        pltpu.VMEM((tm, dff), BF16),
                        pltpu.VMEM((2, tm + H, tf), F32),
                        pltpu.VMEM((2, tm + H, tf), F32)],
        compiler_params=_cparams(1),
        name="conv_ffn",
    )(h, h, g, w_in_bf, conv_w, conv_b, w_out_bf, g_final)


def _rot_lane_layout():
    half = ROT_DIM // 2
    E = ATT_HEAD_DIM
    lay = []
    for base in (0, half):
        lay += [(0, base + i) for i in range(half)] + [(1, base + i) for i in range(half)]
        lay += [(base // half, d) for d in range(ROT_DIM, E)]
    return lay


def _attn_kernel(q_ref, k_ref, v_ref, o_ref, m_ref, acc_a_ref, acc_b_ref,
                 qs_ref, ka_ref, kb_ref, va_ref, vb_ref):
    grp = pl.program_id(2)
    L, lw = q_ref.shape
    T = ATT_BLOCK
    E = ATT_HEAD_DIM
    n_groups = len(ATT_DILATIONS)

    lane1 = lax.broadcasted_iota(jnp.int32, (1, lw), 1)
    head_of_lane = np.array([h for h, _ in _rot_lane_layout()])
    in_a_qk = jnp.zeros((1, lw), jnp.bool_)
    for lo, hi in ((0, 8), (16, 72)):
        in_a_qk = in_a_qk | ((lane1 >= lo) & (lane1 < hi))
    assert all((head_of_lane[l] == 0) == (l < 8 or 16 <= l < 72) for l in range(lw))
    in_a_v = lane1 < E
    qi = lax.broadcasted_iota(jnp.int32, (T, T), 0)
    kj = lax.broadcasted_iota(jnp.int32, (T, T), 1)
    cur_ok = kj <= qi
    both_ok = jnp.concatenate([kj >= qi, cur_ok], axis=1)

    def run_group(gi, dil):
        M = L // dil
        nblk = M // T
        for r in range(dil):
            src = pl.ds(r, M, stride=dil) if dil > 1 else slice(0, M)
            dst = slice(r * M, (r + 1) * M)
            qs_ref[dst, :] = (q_ref[src, :] * (E ** -0.5 * LOG2E)).astype(BF16)
            kf = k_ref[src, :]
            ka_ref[dst, :] = jnp.where(in_a_qk, kf, 0.0).astype(BF16)
            kb_ref[dst, :] = jnp.where(in_a_qk, 0.0, kf).astype(BF16)
            vf = v_ref[src, :]
            va_ref[dst, :] = jnp.where(in_a_v, vf, 1.0).astype(BF16)
            vb_ref[dst, :] = jnp.where(in_a_v, 1.0, vf).astype(BF16)

        blocks = [(r, n) for r in range(dil) for n in range(nblk)]

        def geom(r, n):
            c0 = r * M + n * T
            rows = pl.ds(r + dil * T * n, T, stride=dil) if dil > 1 else slice(n * T, (n + 1) * T)
            kv_rows = slice(c0 - T, c0 + T) if n > 0 else slice(c0, c0 + T)
            return c0, rows, kv_rows

        def scores(r, n):
            c0, _, kv_rows = geom(r, n)
            q = qs_ref[c0:c0 + T, :]
            return _dot_nt(q, ka_ref[kv_rows, :]), _dot_nt(q, kb_ref[kv_rows, :])

        def softmax(r, n, s_a, s_b):
            _, rows, _ = geom(r, n)
            ok = both_ok if n > 0 else cur_ok
            s_a = jnp.where(ok, s_a, NEG_BIG)
            s_b = jnp.where(ok, s_b, NEG_BIG)
            m_a = jnp.max(s_a, axis=-1, keepdims=True)
            m_b = jnp.max(s_b, axis=-1, keepdims=True)
            al_a = al_b = None
            if gi > 0:
                m_old = m_ref[rows, :]
                m_a = jnp.maximum(m_a, m_old[:, 0:1])
                m_b = jnp.maximum(m_b, m_old[:, E:E + 1])
                al_a = jnp.exp2(m_old[:, 0:1] - m_a)
                al_b = jnp.exp2(m_old[:, E:E + 1] - m_b)
            if gi < n_groups - 1:
                m_ref[rows, :] = jnp.where(in_a_v, m_a, m_b)
            return jnp.exp2(s_a - m_a).astype(BF16), jnp.exp2(s_b - m_b).astype(BF16), al_a, al_b

        def values(r, n, p_a, p_b, al_a, al_b):
            _, rows, kv_rows = geom(r, n)
            pv_a = _dot(p_a, va_ref[kv_rows, :])
            pv_b = _dot(p_b, vb_ref[kv_rows, :])
            if gi > 0:
                pv_a = al_a * acc_a_ref[rows, :] + pv_a
                pv_b = al_b * acc_b_ref[rows, :] + pv_b
            if gi < n_groups - 1:
                acc_a_ref[rows, :] = pv_a
                acc_b_ref[rows, :] = pv_b
            else:
                o_ref[rows, :] = jnp.where(in_a_v, pv_a / pltpu.roll(pv_a, E, axis=1),
                                           pv_b / pltpu.roll(pv_b, E, axis=1))

        s_val, p_val = {}, {}
        for step in range(len(blocks) + 2):
            if step < len(blocks):
                s_val[step] = scores(*blocks[step])
            if 0 <= step - 1 < len(blocks):
                p_val[step - 1] = softmax(*blocks[step - 1], *s_val.pop(step - 1))
            if 0 <= step - 2 < len(blocks):
                values(*blocks[step - 2], *p_val.pop(step - 2))

    for gi, dil in enumerate(ATT_DILATIONS):
        pl.when(grp == gi)(functools.partial(run_group, gi, dil))


def _attn(qkv, B, L, gw):
    N = qkv.shape[0]
    lw = 2 * ATT_HEAD_DIM
    npair = gw // lw
    ng = len(ATT_DILATIONS)

    def spec(kind):
        return pl.BlockSpec((L, lw), lambda b, p, g: (b, kind * ng * npair + g * npair + p))

    return pl.pallas_call(
        _attn_kernel,
        out_shape=jax.ShapeDtypeStruct((N, gw), F32),
        grid=(B, npair, ng),
        in_specs=[spec(0), spec(1), spec(2)],
        out_specs=pl.BlockSpec((L, lw), lambda b, p, g: (b, p)),
        scratch_shapes=[pltpu.VMEM((L, lw), F32)] * 3
                       + [pltpu.VMEM((L, lw), BF16)] * 5,
        compiler_params=_cparams(3),
        name="dilated_attention",
    )(qkv, qkv, qkv)


def _s5_params(A_re, A_im, log_dt, B_re, B_im, C_re, C_im, Dd):
    G, P = A_re.shape
    dt = jnp.exp(log_dt)[:, None]
    mag = jnp.exp(A_re * dt)
    ab_re = mag * jnp.cos(A_im * dt)
    ab_im = mag * jnp.sin(A_im * dt)
    den = A_re * A_re + A_im * A_im
    nr, ni = ab_re - 1.0, ab_im
    c_re = (nr * A_re + ni * A_im) / den
    c_im = (ni * A_re - nr * A_im) / den
    Bb_re = c_re[..., None] * B_re - c_im[..., None] * B_im
    Bb_im = c_re[..., None] * B_im + c_im[..., None] * B_re
    hg = G // 2
    eye = jnp.eye(hg, dtype=F32)

    def bd_in(m):
        return jnp.einsum('gpc,gh->gchp', m, eye).reshape(hg * S5_GROUP, hg * P)

    def bd_out(m):
        return jnp.einsum('gcp,gh->gphc', m, eye).reshape(hg * P, hg * S5_GROUP)

    bds, cds = [], []
    for k in range(2):
        sl = slice(k * hg, (k + 1) * hg)
        bds.append(jnp.concatenate([bd_in(Bb_re[sl]), bd_in(Bb_im[sl])], axis=1).astype(BF16))
        cds.append(jnp.concatenate([bd_out(C_re[sl]), -bd_out(C_im[sl])], axis=0).astype(BF16))
    return bds[0], bds[1], ab_re.reshape(1, G * P), ab_im.reshape(1, G * P), cds[0], cds[1], Dd.reshape(1, -1)


ROPE_COLS = 32


def _rope_tables(positions):
    half = ROT_DIM // 2
    inv_freq = ROPE_THETA ** (-jnp.arange(half, dtype=F32) * 2.0 / ROT_DIM)
    ang = positions.astype(F32).reshape(-1, 1) * inv_freq
    n = ang.shape[0]
    cs = jnp.concatenate([jnp.cos(ang), jnp.sin(ang), jnp.ones((n, 1), F32),
                          jnp.zeros((n, ROPE_COLS - 2 * half - 1), F32)], axis=1)
    ex = np.zeros((ROPE_COLS, 2 * LANES), np.float32)
    for lane, (_, d) in enumerate(_rot_lane_layout()):
        if d < half:
            ex[d, lane] = 1.0
            ex[half + d, LANES + lane] = -1.0
        elif d < ROT_DIM:
            ex[d - half, lane] = 1.0
            ex[half + d - half, LANES + lane] = 1.0
        else:
            ex[2 * half, lane] = 1.0
    return cs, jnp.asarray(ex, dtype=BF16)


def _qk_column_perm(n_cols):
    lay = _rot_lane_layout()
    E = ATT_HEAD_DIM
    perm = []
    for slab in range(n_cols // LANES):
        perm += [slab * LANES + h * E + d for h, d in lay]
    return np.array(perm, dtype=np.int32)


def kernel(x, positions, norm_mix, norm_ffn, norm_final, mix_w_in, mix_w_out, s5_A_re, s5_A_im, s5_log_dt, s5_B_re, s5_B_im, s5_C_re, s5_C_im, s5_D, s5_glu_w, s5_glu_b, hgrn_gamma, hgrn_norm, att_w_qkv, att_w_o, ffn_w_in, ffn_conv_w, ffn_conv_b, ffn_w_out):
    B, L, D = x.shape
    N = B * L
    x2 = x.reshape(N, D)
    tm = 512
    s5w = s5_A_re.shape[1] * S5_GROUP
    hgw = hgrn_norm.shape[1]

    proj = _norm_proj(x2, norm_mix[0:1], mix_w_in[0].astype(BF16), tm, s5w, name="inproj")
    s5p = _s5_params(s5_A_re[0], s5_A_im[0], s5_log_dt[0], s5_B_re[0], s5_B_im[0],
                     s5_C_re[0], s5_C_im[0], s5_D[0])
    oa = _s5(proj.reshape(B, L, -1), *s5p, s5_glu_w[0].astype(BF16), s5_glu_b[0:1], s5w)
    lb_all = jnp.cumsum(jax.nn.softmax(hgrn_gamma.astype(F32), axis=0), axis=0)
    ob = _hgrn(proj, lb_all[0:1], hgrn_norm[0:1], B, L, col0=s5w // hgw)
    h = _proj_resid([oa.reshape(N, s5w), ob], mix_w_out[0].astype(BF16), x2, tm)
    h = _ffn(h, norm_ffn[0:1], ffn_w_in[0].astype(BF16), ffn_conv_w[0], ffn_conv_b[0:1],
             ffn_w_out[0].astype(BF16), norm_final.reshape(1, D), L, final_norm=False)

    gw = att_w_o.shape[1]
    n_rot_cols = 2 * len(ATT_DILATIONS) * gw
    w_qkv = att_w_qkv[0]
    w_qkv = jnp.concatenate([w_qkv[:, _qk_column_perm(n_rot_cols)], w_qkv[:, n_rot_cols:]], axis=1)
    qkv = _norm_proj(h, norm_mix[1:2], w_qkv.astype(BF16), tm, gw, rot=_rope_tables(positions),
                     n_rot_cols=n_rot_cols, name="qkv_rotary")
    ao = _attn(qkv, B, L, gw)
    h = _proj_resid([ao], att_w_o[0].astype(BF16), h, tm)
    h = _ffn(h, norm_ffn[1:2], ffn_w_in[1].astype(BF16), ffn_conv_w[1], ffn_conv_b[1:2],
             ffn_w_out[1].astype(BF16), norm_final.reshape(1, D), L, final_norm=True)
    return h.reshape(B, L, D)
```

```python
import functools
import math

import numpy as np
import jax
import jax.numpy as jnp
from jax import lax
from jax.experimental import pallas as pl
from jax.experimental.pallas import tpu as pltpu

F32 = jnp.float32
BF16 = jnp.bfloat16

NORM_EPS = 1e-6
S5_GROUP = 16
S5_STATE = 64
HG_HEADS = 4
HG_BLOCK = 16
ATT_HEAD_DIM = 64
ATT_BLOCK = 128
ATT_DILATIONS = (1, 4, 16)
ROT_DIM = 16
ROPE_THETA = 500000.0
CONV_W = 3
NEG_BIG = -1e30
LOG2E = math.log2(math.e)
LANES = 128

VMEM_LIMIT = 56 * 1024 * 1024


def _cparams(n_axes):
    return pltpu.CompilerParams(dimension_semantics=("arbitrary",) * n_axes,
                                vmem_limit_bytes=VMEM_LIMIT)


def _resident(shape):
    return pl.BlockSpec(shape, lambda *_: (0,) * len(shape), pipeline_mode=pl.Buffered(1))


def _rms(x, g):
    return x * lax.rsqrt(jnp.mean(x * x, axis=-1, keepdims=True) + NORM_EPS) * g


def _sigmoid(x):
    return 1.0 / (1.0 + jnp.exp(-x))


def _dot(a, b):
    return jnp.dot(a, b, preferred_element_type=F32)


def _dot_nt(a, b):
    return lax.dot_general(a, b, (((1,), (1,)), ((), ())), preferred_element_type=F32)


def _dot_tn(a, b):
    return lax.dot_general(a, b, (((0,), (0,)), ((), ())), preferred_element_type=F32)


def _norm_proj_kernel(x_ref, g_ref, w_ref, *rest, tn, n_rot_tiles):
    if n_rot_tiles:
        cs_ref, ex_ref, o_ref, xn_ref = rest
        cs = cs_ref[...]
        cs_hi = cs.astype(BF16)
        cs_lo = (cs - cs_hi.astype(F32)).astype(BF16)
        tab = _dot(cs_hi, ex_ref[...]) + _dot(cs_lo, ex_ref[...])
        c = tab[:, :LANES]
        s = tab[:, LANES:]
    else:
        o_ref, xn_ref = rest
    xn_ref[...] = _rms(x_ref[...], g_ref[...]).astype(BF16)
    for j in range(w_ref.shape[1] // tn):
        y = _dot(xn_ref[...], w_ref[:, j * tn:(j + 1) * tn])
        if j < n_rot_tiles:
            for k in range(tn // LANES):
                t = y[:, k * LANES:(k + 1) * LANES]
                o_ref[:, j * tn + k * LANES:j * tn + (k + 1) * LANES] = (
                    t * c + pltpu.roll(t, LANES // 2, axis=1) * s)
        else:
            o_ref[:, j * tn:(j + 1) * tn] = y


def _norm_proj(x2, g, w_bf, tm, tn, rot=None, n_rot_cols=0, name="norm_proj"):
    N, D = x2.shape
    ncols = w_bf.shape[1]
    in_specs = [pl.BlockSpec((tm, D), lambda i: (i, 0)), _resident((1, D)), _resident(w_bf.shape)]
    args = [x2, g, w_bf]
    if rot is not None:
        cs, expand = rot
        in_specs += [pl.BlockSpec((tm, cs.shape[1]), lambda i: (i, 0)), _resident(expand.shape)]
        args += [cs, expand]
    return pl.pallas_call(
        functools.partial(_norm_proj_kernel, tn=tn, n_rot_tiles=n_rot_cols // tn),
        out_shape=jax.ShapeDtypeStruct((N, ncols), F32),
        grid=(N // tm,),
        in_specs=in_specs,
        out_specs=pl.BlockSpec((tm, ncols), lambda i: (i, 0)),
        scratch_shapes=[pltpu.VMEM((tm, D), BF16)],
        compiler_params=_cparams(1),
        name=name,
    )(*args)


def _s5_kernel(u_ref, bd0_ref, bd1_ref, ar_ref, ai_ref, c0_ref, c1_ref, d_ref, gw_ref, gb_ref,
               o_ref, x0_ref, x1_ref, st_ref, pf_ref, pb_ref, *, lane_chunk):
    nb, ts, W = u_ref.shape
    tm = nb * ts
    half = x0_ref.shape[1] // 2

    @pl.when(pl.program_id(0) == 0)
    def _():
        st_ref[...] = jnp.zeros_like(st_ref)
        r = lax.broadcasted_iota(jnp.int32, (tm, tm), 0)
        c = lax.broadcasted_iota(jnp.int32, (tm, tm), 1)
        pf_ref[...] = jnp.where(c == (r % nb) * ts + r // nb, 1.0, 0.0).astype(BF16)
        pb_ref[...] = jnp.where(r == (c % nb) * ts + c // nb, 1.0, 0.0).astype(BF16)

    u_bm = u_ref[...].reshape(tm, W)
    u_hi = u_bm.astype(BF16)
    u_lo = (u_bm - u_hi.astype(F32)).astype(BF16)
    ut_hi = _dot(pf_ref[...], u_hi)
    u = ut_hi + _dot(pf_ref[...], u_lo)
    ub = ut_hi.astype(BF16)
    ku = W // 2
    x0_ref[...] = _dot(ub[:, :ku], bd0_ref[...])
    x1_ref[...] = _dot(ub[:, ku:], bd1_ref[...])

    for k, x_ref in enumerate((x0_ref, x1_ref)):
        for c in range(half // lane_chunk):
            lo = c * lane_chunk
            re = slice(lo, lo + lane_chunk)
            im = slice(half + lo, half + lo + lane_chunk)
            a_re = jnp.broadcast_to(ar_ref[:, k * half + lo:k * half + lo + lane_chunk], (nb, lane_chunk))
            a_im = jnp.broadcast_to(ai_ref[:, k * half + lo:k * half + lo + lane_chunk], (nb, lane_chunk))

            def body(s, carry, x_ref=x_ref, re=re, im=im, a_re=a_re, a_im=a_im):
                xr, xi = carry
                rows = pl.ds(pl.multiple_of(s * nb, nb), nb)
                nr = a_re * xr - a_im * xi + x_ref[rows, re]
                ni = a_re * xi + a_im * xr + x_ref[rows, im]
                x_ref[rows, re] = nr
                x_ref[rows, im] = ni
                return nr, ni

            xr, xi = lax.fori_loop(0, ts, body, (st_ref[2 * k, :, re], st_ref[2 * k + 1, :, re]),
                                   unroll=8)
            st_ref[2 * k, :, re] = xr
            st_ref[2 * k + 1, :, re] = xi

    y0 = _dot(x0_ref[...].astype(BF16), c0_ref[...])
    y1 = _dot(x1_ref[...].astype(BF16), c1_ref[...])
    y = jnp.concatenate([y0, y1], axis=1) + d_ref[...] * u
    z = 0.5 * y * (1.0 + jnp.tanh(math.sqrt(2.0 / math.pi) * (y + 0.044715 * (y * y * y))))
    gate = _sigmoid(_dot(z.astype(BF16), gw_ref[...]) + gb_ref[...])
    o_tm = (z * gate).astype(BF16)
    o_ref[...] = _dot(pb_ref[...], o_tm).astype(BF16).reshape(nb, ts, W)


def _s5(proj3, bd0, bd1, ar, ai, c0, c1, dd, gw, gb, W, ts=64):
    B, L, _ = proj3.shape
    S2 = bd0.shape[1]
    tm = B * ts
    kern = functools.partial(_s5_kernel, lane_chunk=512)
    return pl.pallas_call(
        kern,
        out_shape=jax.ShapeDtypeStruct((B, L, W), BF16),
        grid=(L // ts,),
        in_specs=[pl.BlockSpec((B, ts, W), lambda t: (0, t, 0))]
                 + [_resident(a.shape) for a in (bd0, bd1, ar, ai, c0, c1, dd, gw, gb)],
        out_specs=pl.BlockSpec((B, ts, W), lambda t: (0, t, 0)),
        scratch_shapes=[pltpu.VMEM((tm, S2), F32), pltpu.VMEM((tm, S2), F32),
                        pltpu.VMEM((4, B, S2 // 2), F32),
                        pltpu.VMEM((tm, tm), BF16), pltpu.VMEM((tm, tm), BF16)],
        compiler_params=_cparams(1),
        name="s5_mixer",
    )(proj3, bd0, bd1, ar, ai, c0, c1, dd, gw, gb)


def _hgrn_kernel(xq_ref, xf_ref, xi_ref, xg_ref, lb_ref, ng_ref, o_ref,
                 st_ref, qe_ref, ke_ref, eb_ref, q_ref, k_ref, f_ref, oi_ref):
    tl, W = xq_ref.shape
    C = HG_BLOCK
    hd = W // HG_HEADS
    nblk = tl // C

    @pl.when(pl.program_id(1) == 0)
    def _():
        st_ref[...] = jnp.zeros_like(st_ref)

    xq = xq_ref[...]
    q = xq * _sigmoid(xq)
    lb = lb_ref[...]
    f = lb + (1.0 - lb) * _sigmoid(xf_ref[...])
    kk = 1.0 - f
    g2 = jnp.log(f) * LOG2E

    row = lax.broadcasted_iota(jnp.int32, (tl, tl), 0)
    col = lax.broadcasted_iota(jnp.int32, (tl, tl), 1)
    same = (row // C) == (col // C)
    t_cum = jnp.where(same & (col <= row), 1.0, 0.0).astype(BF16)
    t_blk = jnp.where(same, 1.0, 0.0).astype(BF16)
    g_hi = g2.astype(BF16)
    g_lo = (g2 - g_hi.astype(F32)).astype(BF16)
    b = _dot(t_cum, g_hi) + _dot(t_cum, g_lo)
    bl = _dot(t_blk, g_hi) + _dot(t_blk, g_lo)

    q_ref[...] = q
    k_ref[...] = kk
    f_ref[...] = f
    qe_ref[...] = (q * jnp.exp2(b)).astype(BF16)
    ke_ref[...] = (kk * jnp.exp2(bl - b)).astype(BF16)
    eb_ref[...] = jnp.exp2(bl)

    heads = [slice(h * hd, (h + 1) * hd) for h in range(HG_HEADS)]
    st = [st_ref[h] for h in range(HG_HEADS)]
    for i in range(nblk):
        rows = slice(i * C, (i + 1) * C)
        for h, hl in enumerate(heads):
            oi_ref[rows, hl] = _dot_nt(qe_ref[rows, hl], st[h].astype(BF16))
            vb = xi_ref[rows, hl].astype(BF16)
            st[h] = st[h] * eb_ref[i * C:i * C + 1, hl] + _dot_tn(vb, ke_ref[rows, hl])
    for h in range(HG_HEADS):
        st_ref[h] = st[h]

    rc = 64 if tl % 64 == 0 else tl
    r_i = lax.broadcasted_iota(jnp.int32, (rc, tl), 0)
    c_i = lax.broadcasted_iota(jnp.int32, (rc, tl), 1)
    for r0 in range(0, tl, rc):
        lag = jnp.where(((r_i + r0) // C) == (c_i // C), r_i + r0 - c_i, -1)
        rows = slice(r0, r0 + rc)
        p = [k_ref[rows, hl] for hl in heads]
        a_mat = [jnp.zeros((rc, tl), F32) for _ in heads]
        for d in range(C):
            for h, hl in enumerate(heads):
                if d > 0:
                    p[h] = f_ref[rows, hl] * pltpu.roll(p[h], 1, axis=0)
                a = jnp.sum(q_ref[rows, hl] * p[h], axis=-1, keepdims=True)
                a_mat[h] = jnp.where(lag == d, a, a_mat[h])
        for h, hl in enumerate(heads):
            acc = oi_ref[rows, hl] + _dot(a_mat[h].astype(BF16), xi_ref[:, hl].astype(BF16))
            on = acc * lax.rsqrt(jnp.mean(acc * acc, axis=-1, keepdims=True) + NORM_EPS)
            xg = xg_ref[rows, hl]
            o_ref[rows, hl] = (on * ng_ref[:, hl] * (xg * _sigmoid(xg))).astype(o_ref.dtype)


def _hgrn(proj, lb, ng, B, L, col0, tl=128):
    N = proj.shape[0]
    W = lb.shape[1]
    hd = W // HG_HEADS
    tps = L // tl
    col = lambda c: pl.BlockSpec((tl, W), lambda b, t: (b * tps + t, col0 + c))
    return pl.pallas_call(
        _hgrn_kernel,
        out_shape=jax.ShapeDtypeStruct((N, W), BF16),
        grid=(B, tps),
        in_specs=[col(0), col(1), col(2), col(3), _resident((1, W)), _resident((1, W))],
        out_specs=pl.BlockSpec((tl, W), lambda b, t: (b * tps + t, 0)),
        scratch_shapes=[pltpu.VMEM((HG_HEADS, hd, hd), F32),
                        pltpu.VMEM((tl, W), BF16),
                        pltpu.VMEM((tl, W), BF16),
                        pltpu.VMEM((tl, W), F32),
                        pltpu.VMEM((tl, W), F32),
                        pltpu.VMEM((tl, W), F32),
                        pltpu.VMEM((tl, W), F32),
                        pltpu.VMEM((tl, W), F32)],
        compiler_params=_cparams(2),
        name="hgrn2_mixer",
    )(proj, proj, proj, proj, lb, ng)


FFN_HALO = 16


def _ffn_kernel(*refs, n_mix, tiles_per_seq, final_norm, tf):
    x_ref, xhalo_ref = refs[:2]
    mix_refs = refs[2:2 + n_mix]
    mixhalo_refs = refs[2 + n_mix:2 + 2 * n_mix]
    (wp_ref, g_ref, win_ref, cw_ref, cb_ref, wo_ref, gf_ref, o_ref,
     xn_ref, act_ref, a_ref, b_ref, h_ref) = refs[2 + 2 * n_mix:]
    i = pl.program_id(0)
    tm = x_ref.shape[0]
    H = FFN_HALO
    dff = wo_ref.shape[0]

    def mixed(r_ref, m_refs):
        acc = r_ref[...]
        k0 = 0
        for m_ref in m_refs:
            kw = m_ref.shape[1]
            acc = acc + _dot(m_ref[...].astype(BF16), wp_ref[k0:k0 + kw, :])
            k0 += kw
        return acc

    h_ref[...] = mixed(x_ref, mix_refs)
    xn_ref[H:, :] = _rms(h_ref[...], g_ref[...]).astype(BF16)
    hn = _rms(mixed(xhalo_ref, mixhalo_refs), g_ref[...])
    xn_ref[0:H, :] = jnp.where(i % tiles_per_seq == 0, 0.0, hn).astype(BF16)

    def conv(s_ref, c0):
        cols = slice(c0, c0 + tf)
        out = cb_ref[:, cols] + s_ref[H:H + tm, :] * cw_ref[CONV_W - 1:CONV_W, cols]
        for t in range(1, CONV_W):
            out = out + s_ref[H - t:H - t + tm, :] * cw_ref[CONV_W - 1 - t:CONV_W - t, cols]
        return out

    for j in range(dff // tf):
        sa = a_ref.at[j % 2]
        sb = b_ref.at[j % 2]
        sa[...] = _dot(xn_ref[...], win_ref[:, j * tf:(j + 1) * tf])
        sb[...] = _dot(xn_ref[...], win_ref[:, dff + j * tf:dff + (j + 1) * tf])
        a = conv(sa, j * tf)
        b = conv(sb, dff + j * tf)
        act_ref[:, j * tf:(j + 1) * tf] = (a * _sigmoid(a) * b).astype(BF16)

    y = h_ref[...] + _dot(act_ref[...], wo_ref[...])
    if final_norm:
        y = _rms(y, gf_ref[...])
    o_ref[...] = y


def _mix_ffn(x, mix_list, w_proj_bf, g, w_in_bf, conv_w, conv_b, w_out_bf, g_final, L, final_norm,
             tm=512, tf=256):
    N, D = x.shape
    dff = w_out_bf.shape[0]
    H = FFN_HALO
    main = lambda w: pl.BlockSpec((tm, w), lambda i: (i, 0))
    halo = lambda w: pl.BlockSpec((H, w), lambda i: (jnp.maximum(i * (tm // H) - 1, 0), 0))
    kern = functools.partial(_ffn_kernel, n_mix=len(mix_list), tiles_per_seq=L // tm,
                             final_norm=final_norm, tf=tf)
    return pl.pallas_call(
        kern,
        out_shape=jax.ShapeDtypeStruct((N, D), F32),
        grid=(N // tm,),
        in_specs=[main(D), halo(D)]
                 + [main(m.shape[1]) for m in mix_list] + [halo(m.shape[1]) for m in mix_list]
                 + [_resident(w_proj_bf.shape), _resident((1, D)), _resident(w_in_bf.shape),
                    _resident(conv_w.shape), _resident(conv_b.shape), _resident(w_out_bf.shape),
                    _resident((1, D))],
        out_specs=pl.BlockSpec((tm, D), lambda i: (i, 0)),
        scratch_shapes=[pltpu.VMEM((tm + H, D), BF16),
                        pltpu.VMEM((tm, dff), BF16),
                        pltpu.VMEM((2, tm + H, tf), F32),
                        pltpu.VMEM((2, tm + H, tf), F32),
                        pltpu.VMEM((tm, D), F32)],
        compiler_params=_cparams(1),
        name="mix_conv_ffn",
    )(x, x, *mix_list, *mix_list, w_proj_bf, g, w_in_bf, conv_w, conv_b, w_out_bf, g_final)


def _rot_lane_layout():
    half = ROT_DIM // 2
    E = ATT_HEAD_DIM
    lay = []
    for base in (0, half):
        lay += [(0, base + i) for i in range(half)] + [(1, base + i) for i in range(half)]
        lay += [(base // half, d) for d in range(ROT_DIM, E)]
    return lay


def _attn_kernel(q_ref, k_ref, v_ref, o_ref, m_a_ref, m_b_ref, acc_a_ref, acc_b_ref, tmp_ref, stage_ref,
                 qs_ref, ka_ref, kb_ref, va_ref, vb_ref):
    grp = pl.program_id(2)
    L, lw = q_ref.shape
    T = ATT_BLOCK
    E = ATT_HEAD_DIM
    n_groups = len(ATT_DILATIONS)

    lane1 = lax.broadcasted_iota(jnp.int32, (1, lw), 1)
    head_of_lane = np.array([h for h, _ in _rot_lane_layout()])
    in_a_qk = jnp.zeros((1, lw), jnp.bool_)
    for lo, hi in ((0, 8), (16, 72)):
        in_a_qk = in_a_qk | ((lane1 >= lo) & (lane1 < hi))
    assert all((head_of_lane[l] == 0) == (l < 8 or 16 <= l < 72) for l in range(lw))
    in_a_v = lane1 < E
    qi = lax.broadcasted_iota(jnp.int32, (T, T), 0)
    kj = lax.broadcasted_iota(jnp.int32, (T, T), 1)
    cur_ok = kj <= qi
    both_ok = jnp.concatenate([kj >= qi, cur_ok], axis=1)

    S = ATT_DILATIONS[1]
    assert ATT_DILATIONS == (1, S, S * S) and T % S == 0
    Ls = L // S

    def by_s_rows(dil, r, n):
        if dil == S:
            return slice(r * Ls + n * T, r * Ls + (n + 1) * T)
        return pl.ds((r % S) * Ls + r // S, T, stride=S)

    def run_group(gi, dil):
        M = L // dil
        nblk = M // T

        def residue_major(x_ref, emit):
            if dil == S * S:
                for r4 in range(S):
                    tmp_ref[r4 * Ls:(r4 + 1) * Ls, :] = x_ref[pl.ds(r4, Ls, stride=S), :]
            for r in range(dil):
                if dil == 1:
                    xr = x_ref[...]
                elif dil == S:
                    xr = x_ref[pl.ds(r, M, stride=S), :]
                else:
                    xr = tmp_ref[by_s_rows(dil, r, 0), :]
                emit(slice(r * M, (r + 1) * M), xr)

        def emit_q(dst, xr):
            qs_ref[dst, :] = (xr * (E ** -0.5 * LOG2E)).astype(BF16)

        def emit_k(dst, xr):
            ka_ref[dst, :] = jnp.where(in_a_qk, xr, 0.0).astype(BF16)
            kb_ref[dst, :] = jnp.where(in_a_qk, 0.0, xr).astype(BF16)

        def emit_v(dst, xr):
            va_ref[dst, :] = jnp.where(in_a_v, xr, 1.0).astype(BF16)
            vb_ref[dst, :] = jnp.where(in_a_v, 1.0, xr).astype(BF16)

        residue_major(q_ref, emit_q)
        residue_major(k_ref, emit_k)
        residue_major(v_ref, emit_v)

        blocks = [(r, n) for r in range(dil) for n in range(nblk)]

        def load_state(ref, r, n):
            return ref[by_s_rows(dil, r, n), :]

        def store_state(ref, which, r, n, val):
            if dil == 1:
                stage_ref[which] = val
                for j in range(S):
                    ref[j * Ls + (T // S) * n:j * Ls + (T // S) * (n + 1), :] = (
                        stage_ref[which, pl.ds(j, T // S, stride=S), :])
            else:
                ref[by_s_rows(dil, r, n), :] = val

        def kv_rows(r, n):
            c0 = r * M + n * T
            return slice(c0 - T, c0 + T) if n > 0 else slice(c0, c0 + T)

        def scores(r, n):
            c0 = r * M + n * T
            q = qs_ref[c0:c0 + T, :]
            return _dot_nt(q, ka_ref[kv_rows(r, n), :]), _dot_nt(q, kb_ref[kv_rows(r, n), :])

        def softmax(r, n, s_a, s_b):
            ok = both_ok if n > 0 else cur_ok
            s_a = jnp.where(ok, s_a, NEG_BIG)
            s_b = jnp.where(ok, s_b, NEG_BIG)
            m_a = jnp.broadcast_to(jnp.max(s_a, axis=-1, keepdims=True), (T, lw))
            m_b = jnp.broadcast_to(jnp.max(s_b, axis=-1, keepdims=True), (T, lw))
            al_a = al_b = None
            if gi > 0:
                m_old_a = load_state(m_a_ref, r, n)
                m_old_b = load_state(m_b_ref, r, n)
                m_a = jnp.maximum(m_a, m_old_a)
                m_b = jnp.maximum(m_b, m_old_b)
                al_a = jnp.exp2(m_old_a - m_a)
                al_b = jnp.exp2(m_old_b - m_b)
            if gi < n_groups - 1:
                store_state(m_a_ref, 0, r, n, m_a)
                store_state(m_b_ref, 3, r, n, m_b)
            if n > 0:
                m_a = jnp.concatenate([m_a, m_a], axis=1)
                m_b = jnp.concatenate([m_b, m_b], axis=1)
            return jnp.exp2(s_a - m_a).astype(BF16), jnp.exp2(s_b - m_b).astype(BF16), al_a, al_b

        def values(r, n, p_a, p_b, al_a, al_b):
            pv_a = _dot(p_a, va_ref[kv_rows(r, n), :])
            pv_b = _dot(p_b, vb_ref[kv_rows(r, n), :])
            if gi > 0:
                pv_a = al_a * load_state(acc_a_ref, r, n) + pv_a
                pv_b = al_b * load_state(acc_b_ref, r, n) + pv_b
            if gi < n_groups - 1:
                store_state(acc_a_ref, 1, r, n, pv_a)
                store_state(acc_b_ref, 2, r, n, pv_b)
            else:
                tmp_ref[by_s_rows(dil, r, n), :] = jnp.where(
                    in_a_v, pv_a / pltpu.roll(pv_a, E, axis=1), pv_b / pltpu.roll(pv_b, E, axis=1))

        s_val, p_val = {}, {}
        for step in range(len(blocks) + 2):
            if step < len(blocks):
                s_val[step] = scores(*blocks[step])
            if 0 <= step - 1 < len(blocks):
                p_val[step - 1] = softmax(*blocks[step - 1], *s_val.pop(step - 1))
            if 0 <= step - 2 < len(blocks):
                values(*blocks[step - 2], *p_val.pop(step - 2))

        if gi == n_groups - 1:
            for r4 in range(S):
                o_ref[pl.ds(r4, Ls, stride=S), :] = tmp_ref[r4 * Ls:(r4 + 1) * Ls, :]

    for gi, dil in enumerate(ATT_DILATIONS):
        pl.when(grp == gi)(functools.partial(run_group, gi, dil))


def _attn(qkv, B, L, gw):
    N = qkv.shape[0]
    lw = 2 * ATT_HEAD_DIM
    npair = gw // lw
    ng = len(ATT_DILATIONS)

    def spec(kind):
        return pl.BlockSpec((L, lw), lambda b, p, g: (b, kind * ng * npair + g * npair + p))

    return pl.pallas_call(
        _attn_kernel,
        out_shape=jax.ShapeDtypeStruct((N, gw), F32),
        grid=(B, npair, ng),
        in_specs=[spec(0), spec(1), spec(2)],
        out_specs=pl.BlockSpec((L, lw), lambda b, p, g: (b, p)),
        scratch_shapes=[pltpu.VMEM((L, lw), F32)] * 4
                       + [pltpu.VMEM((L, lw), F32),
                          pltpu.VMEM((4, ATT_BLOCK, lw), F32)]
                       + [pltpu.VMEM((L, lw), BF16)] * 5,
        compiler_params=_cparams(3),
        name="dilated_attention",
    )(qkv, qkv, qkv)


def _s5_params(A_re, A_im, log_dt, B_re, B_im, C_re, C_im, Dd):
    G, P = A_re.shape
    dt = jnp.exp(log_dt)[:, None]
    mag = jnp.exp(A_re * dt)
    ab_re = mag * jnp.cos(A_im * dt)
    ab_im = mag * jnp.sin(A_im * dt)
    den = A_re * A_re + A_im * A_im
    nr, ni = ab_re - 1.0, ab_im
    c_re = (nr * A_re + ni * A_im) / den
    c_im = (ni * A_re - nr * A_im) / den
    Bb_re = c_re[..., None] * B_re - c_im[..., None] * B_im
    Bb_im = c_re[..., None] * B_im + c_im[..., None] * B_re
    hg = G // 2
    eye = jnp.eye(hg, dtype=F32)

    def bd_in(m):
        return jnp.einsum('gpc,gh->gchp', m, eye).reshape(hg * S5_GROUP, hg * P)

    def bd_out(m):
        return jnp.einsum('gcp,gh->gphc', m, eye).reshape(hg * P, hg * S5_GROUP)

    bds, cds = [], []
    for k in range(2):
        sl = slice(k * hg, (k + 1) * hg)
        bds.append(jnp.concatenate([bd_in(Bb_re[sl]), bd_in(Bb_im[sl])], axis=1).astype(BF16))
        cds.append(jnp.concatenate([bd_out(C_re[sl]), -bd_out(C_im[sl])], axis=0).astype(BF16))
    return bds[0], bds[1], ab_re.reshape(1, G * P), ab_im.reshape(1, G * P), cds[0], cds[1], Dd.reshape(1, -1)


ROPE_COLS = 32


def _rope_tables(positions):
    half = ROT_DIM // 2
    inv_freq = ROPE_THETA ** (-jnp.arange(half, dtype=F32) * 2.0 / ROT_DIM)
    ang = positions.astype(F32).reshape(-1, 1) * inv_freq
    n = ang.shape[0]
    cs = jnp.concatenate([jnp.cos(ang), jnp.sin(ang), jnp.ones((n, 1), F32),
                          jnp.zeros((n, ROPE_COLS - 2 * half - 1), F32)], axis=1)
    ex = np.zeros((ROPE_COLS, 2 * LANES), np.float32)
    for lane, (_, d) in enumerate(_rot_lane_layout()):
        if d < half:
            ex[d, lane] = 1.0
            ex[half + d, LANES + lane] = -1.0
        elif d < ROT_DIM:
            ex[d - half, lane] = 1.0
            ex[half + d - half, LANES + lane] = 1.0
        else:
            ex[2 * half, lane] = 1.0
    return cs, jnp.asarray(ex, dtype=BF16)


def _qk_lane_layout(w):
    rows, n_cols = w.shape
    E = ATT_HEAD_DIM
    wp = w.reshape(rows, n_cols // LANES, 2, E)
    lay = _rot_lane_layout()
    pieces, start = [], 0
    for l in range(1, LANES + 1):
        if l == LANES or lay[l][0] != lay[start][0] or lay[l][1] != lay[l - 1][1] + 1:
            h, d0 = lay[start]
            pieces.append(wp[:, :, h, d0:d0 + (l - start)])
            start = l
    return jnp.concatenate(pieces, axis=-1).reshape(rows, n_cols)


def kernel(x, positions, norm_mix, norm_ffn, norm_final, mix_w_in, mix_w_out, s5_A_re, s5_A_im, s5_log_dt, s5_B_re, s5_B_im, s5_C_re, s5_C_im, s5_D, s5_glu_w, s5_glu_b, hgrn_gamma, hgrn_norm, att_w_qkv, att_w_o, ffn_w_in, ffn_conv_w, ffn_conv_b, ffn_w_out):
    B, L, D = x.shape
    N = B * L
    x2 = x.reshape(N, D)
    tm = 512
    s5w = s5_A_re.shape[1] * S5_GROUP
    hgw = hgrn_norm.shape[1]

    proj = _norm_proj(x2, norm_mix[0:1], mix_w_in[0].astype(BF16), tm, s5w, name="inproj")
    s5p = _s5_params(s5_A_re[0], s5_A_im[0], s5_log_dt[0], s5_B_re[0], s5_B_im[0],
                     s5_C_re[0], s5_C_im[0], s5_D[0])
    oa = _s5(proj.reshape(B, L, -1), *s5p, s5_glu_w[0].astype(BF16), s5_glu_b[0:1], s5w)
    lb_all = jnp.cumsum(jax.nn.softmax(hgrn_gamma.astype(F32), axis=0), axis=0)
    ob = _hgrn(proj, lb_all[0:1], hgrn_norm[0:1], B, L, col0=s5w // hgw)
    h = _mix_ffn(x2, [oa.reshape(N, s5w), ob], mix_w_out[0].astype(BF16), norm_ffn[0:1],
                 ffn_w_in[0].astype(BF16), ffn_conv_w[0], ffn_conv_b[0:1], ffn_w_out[0].astype(BF16),
                 norm_final.reshape(1, D), L, final_norm=False)

    gw = att_w_o.shape[1]
    n_rot_cols = 2 * len(ATT_DILATIONS) * gw
    w_qkv = att_w_qkv[0].astype(BF16)
    w_qkv = jnp.concatenate([_qk_lane_layout(w_qkv[:, :n_rot_cols]), w_qkv[:, n_rot_cols:]], axis=1)
    qkv = _norm_proj(h, norm_mix[1:2], w_qkv, tm, gw, rot=_rope_tables(positions),
                     n_rot_cols=n_rot_cols, name="qkv_rotary")
    ao = _attn(qkv, B, L, gw)
    h = _mix_ffn(h, [ao], att_w_o[0].astype(BF16), norm_ffn[1:2],
                 ffn_w_in[1].astype(BF16), ffn_conv_w[1], ffn_conv_b[1:2], ffn_w_out[1].astype(BF16),
                 norm_final.reshape(1, D), L, final_norm=True)
    return h.reshape(B, L, D)
```

```python
import functools
import math

import numpy as np
import jax
import jax.numpy as jnp
from jax import lax
from jax.experimental import pallas as pl
from jax.experimental.pallas import tpu as pltpu

F32 = jnp.float32
BF16 = jnp.bfloat16

NORM_EPS = 1e-6
S5_GROUP = 16
S5_STATE = 64
HG_HEADS = 4
HG_BLOCK = 128
HG_LAG_BLOCK = 4
ATT_HEAD_DIM = 64
ATT_BLOCK = 128
ATT_DILATIONS = (1, 4, 16)
ROT_DIM = 16
ROPE_THETA = 500000.0
CONV_W = 3
NEG_BIG = -1e30
LOG2E = math.log2(math.e)
LANES = 128

VMEM_LIMIT = 56 * 1024 * 1024


def _cparams(n_axes):
    return pltpu.CompilerParams(dimension_semantics=("arbitrary",) * n_axes,
                                vmem_limit_bytes=VMEM_LIMIT)


def _resident(shape):
    return pl.BlockSpec(shape, lambda *_: (0,) * len(shape), pipeline_mode=pl.Buffered(1))


def _resident_layer(shape, layer):
    return pl.BlockSpec((None,) + tuple(shape[1:]), lambda *_: (layer,) + (0,) * (len(shape) - 1),
                        pipeline_mode=pl.Buffered(1))


def _rms(x, g):
    return x * lax.rsqrt(jnp.mean(x * x, axis=-1, keepdims=True) + NORM_EPS) * g


def _sigmoid(x):
    return 1.0 / (1.0 + jnp.exp(-x))


def _dot(a, b):
    return jnp.dot(a, b, preferred_element_type=F32)


def _dot_nt(a, b):
    return lax.dot_general(a, b, (((1,), (1,)), ((), ())), preferred_element_type=F32)


def _dot_tn(a, b):
    return lax.dot_general(a, b, (((0,), (0,)), ((), ())), preferred_element_type=F32)


def _norm_proj_kernel(x_ref, g_ref, w_ref, *rest, tn, n_rot_tiles):
    if n_rot_tiles:
        cs_ref, ex_ref, o_ref, xn_ref = rest
        cs = cs_ref[...]
        cs_hi = cs.astype(BF16)
        cs_lo = (cs - cs_hi.astype(F32)).astype(BF16)
        tab = _dot(cs_hi, ex_ref[...]) + _dot(cs_lo, ex_ref[...])
        c = tab[:, :LANES]
        s = tab[:, LANES:]
    else:
        o_ref, xn_ref = rest
    xn_ref[...] = _rms(x_ref[...], g_ref[...]).astype(BF16)
    for j in range(w_ref.shape[1] // tn):
        y = _dot(xn_ref[...], w_ref[:, j * tn:(j + 1) * tn])
        if j < n_rot_tiles:
            for k in range(tn // LANES):
                t = y[:, k * LANES:(k + 1) * LANES]
                o_ref[:, j * tn + k * LANES:j * tn + (k + 1) * LANES] = (
                    t * c + pltpu.roll(t, LANES // 2, axis=1) * s)
        else:
            o_ref[:, j * tn:(j + 1) * tn] = y


def _norm_proj(x2, g, w_bf, tm, tn, rot=None, n_rot_cols=0, name="norm_proj"):
    N, D = x2.shape
    ncols = w_bf.shape[1]
    in_specs = [pl.BlockSpec((tm, D), lambda i: (i, 0)), _resident((1, D)), _resident(w_bf.shape)]
    args = [x2, g, w_bf]
    if rot is not None:
        cs, expand = rot
        in_specs += [pl.BlockSpec((tm, cs.shape[1]), lambda i: (i, 0)), _resident(expand.shape)]
        args += [cs, expand]
    return pl.pallas_call(
        functools.partial(_norm_proj_kernel, tn=tn, n_rot_tiles=n_rot_cols // tn),
        out_shape=jax.ShapeDtypeStruct((N, ncols), F32),
        grid=(N // tm,),
        in_specs=in_specs,
        out_specs=pl.BlockSpec((tm, ncols), lambda i: (i, 0)),
        scratch_shapes=[pltpu.VMEM((tm, D), BF16)],
        compiler_params=_cparams(1),
        name=name,
    )(*args)


def _s5_kernel(u_ref, bd0_ref, bd1_ref, ar_ref, ai_ref, c0_ref, c1_ref, d_ref, gw_ref, gb_ref,
               o_ref, x0_ref, x1_ref, st_ref, pf_ref, pb_ref, *, lane_chunk):
    nb, ts, W = u_ref.shape
    tm = nb * ts
    half = x0_ref.shape[1] // 2

    @pl.when(pl.program_id(0) == 0)
    def _():
        st_ref[...] = jnp.zeros_like(st_ref)
        r = lax.broadcasted_iota(jnp.int32, (tm, tm), 0)
        c = lax.broadcasted_iota(jnp.int32, (tm, tm), 1)
        pf_ref[...] = jnp.where(c == (r % nb) * ts + r // nb, 1.0, 0.0).astype(BF16)
        pb_ref[...] = jnp.where(r == (c % nb) * ts + c // nb, 1.0, 0.0).astype(BF16)

    u_bm = u_ref[...].reshape(tm, W)
    u_hi = u_bm.astype(BF16)
    u_lo = (u_bm - u_hi.astype(F32)).astype(BF16)
    ut_hi = _dot(pf_ref[...], u_hi)
    u = ut_hi + _dot(pf_ref[...], u_lo)
    ub = ut_hi.astype(BF16)
    ku = W // 2
    x0_ref[...] = _dot(ub[:, :ku], bd0_ref[...])
    x1_ref[...] = _dot(ub[:, ku:], bd1_ref[...])

    for k, x_ref in enumerate((x0_ref, x1_ref)):
        for c in range(half // lane_chunk):
            lo = c * lane_chunk
            re = slice(lo, lo + lane_chunk)
            im = slice(half + lo, half + lo + lane_chunk)
            a_re = jnp.broadcast_to(ar_ref[:, k * half + lo:k * half + lo + lane_chunk], (nb, lane_chunk))
            a_im = jnp.broadcast_to(ai_ref[:, k * half + lo:k * half + lo + lane_chunk], (nb, lane_chunk))

            def body(s, carry, x_ref=x_ref, re=re, im=im, a_re=a_re, a_im=a_im):
                xr, xi = carry
                rows = pl.ds(pl.multiple_of(s * nb, nb), nb)
                nr = a_re * xr - a_im * xi + x_ref[rows, re]
                ni = a_re * xi + a_im * xr + x_ref[rows, im]
                x_ref[rows, re] = nr
                x_ref[rows, im] = ni
                return nr, ni

            xr, xi = lax.fori_loop(0, ts, body, (st_ref[2 * k, :, re], st_ref[2 * k + 1, :, re]),
                                   unroll=8)
            st_ref[2 * k, :, re] = xr
            st_ref[2 * k + 1, :, re] = xi

    y0 = _dot(x0_ref[...].astype(BF16), c0_ref[...])
    y1 = _dot(x1_ref[...].astype(BF16), c1_ref[...])
    y = jnp.concatenate([y0, y1], axis=1) + d_ref[...] * u
    z = 0.5 * y * (1.0 + jnp.tanh(math.sqrt(2.0 / math.pi) * (y + 0.044715 * (y * y * y))))
    gate = _sigmoid(_dot(z.astype(BF16), gw_ref[...]) + gb_ref[...])
    o_tm = (z * gate).astype(BF16)
    o_ref[...] = _dot(pb_ref[...], o_tm).astype(BF16).reshape(nb, ts, W)


def _s5(proj3, bd0, bd1, ar, ai, c0, c1, dd, gw, gb, W, ts=64):
    B, L, _ = proj3.shape
    S2 = bd0.shape[1]
    tm = B * ts
    kern = functools.partial(_s5_kernel, lane_chunk=1024)
    return pl.pallas_call(
        kern,
        out_shape=jax.ShapeDtypeStruct((B, L, W), BF16),
        grid=(L // ts,),
        in_specs=[pl.BlockSpec((B, ts, W), lambda t: (0, t, 0))]
                 + [_resident(a.shape) for a in (bd0, bd1, ar, ai, c0, c1, dd, gw, gb)],
        out_specs=pl.BlockSpec((B, ts, W), lambda t: (0, t, 0)),
        scratch_shapes=[pltpu.VMEM((tm, S2), F32), pltpu.VMEM((tm, S2), F32),
                        pltpu.VMEM((4, B, S2 // 2), F32),
                        pltpu.VMEM((tm, tm), BF16), pltpu.VMEM((tm, tm), BF16)],
        compiler_params=_cparams(1),
        name="s5_mixer",
    )(proj3, bd0, bd1, ar, ai, c0, c1, dd, gw, gb)


def _hgrn_kernel(xq_ref, xf_ref, xi_ref, xg_ref, lb_ref, ng_ref, o_ref,
                 st_ref, qe_ref, ke_ref, eb_ref, q_ref, k_ref, f_ref, b_ref, oi_ref):
    tl, W = xq_ref.shape
    C = HG_BLOCK
    hd = W // HG_HEADS
    nblk = tl // C

    @pl.when(pl.program_id(1) == 0)
    def _():
        st_ref[...] = jnp.zeros_like(st_ref)

    xq = xq_ref[...]
    q = xq * _sigmoid(xq)
    lb = lb_ref[...]
    f = lb + (1.0 - lb) * _sigmoid(xf_ref[...])
    kk = 1.0 - f
    g2 = jnp.log(f) * LOG2E

    row = lax.broadcasted_iota(jnp.int32, (tl, tl), 0)
    col = lax.broadcasted_iota(jnp.int32, (tl, tl), 1)
    same = (row // C) == (col // C)
    t_cum = jnp.where(same & (col <= row), 1.0, 0.0).astype(BF16)
    t_blk = jnp.where(same, 1.0, 0.0).astype(BF16)
    g_hi = g2.astype(BF16)
    g_lo = (g2 - g_hi.astype(F32)).astype(BF16)
    b = _dot(t_cum, g_hi) + _dot(t_cum, g_lo)
    bl = _dot(t_blk, g_hi) + _dot(t_blk, g_lo)

    q_ref[...] = q
    k_ref[...] = kk
    f_ref[...] = f
    b_ref[...] = b
    qe_ref[...] = (q * jnp.exp2(b)).astype(BF16)
    ke_ref[...] = (kk * jnp.exp2(bl - b)).astype(BF16)
    eb_ref[...] = jnp.exp2(bl)

    heads = [slice(h * hd, (h + 1) * hd) for h in range(HG_HEADS)]
    st = [st_ref[h] for h in range(HG_HEADS)]
    for i in range(nblk):
        rows = slice(i * C, (i + 1) * C)
        for h, hl in enumerate(heads):
            oi_ref[rows, hl] = _dot_nt(qe_ref[rows, hl], st[h].astype(BF16))
            vb = xi_ref[rows, hl].astype(BF16)
            st[h] = st[h] * eb_ref[i * C:i * C + 1, hl] + _dot_tn(vb, ke_ref[rows, hl])
    for h in range(HG_HEADS):
        st_ref[h] = st[h]

    Cl = HG_LAG_BLOCK
    levels = []
    size = 2 * Cl
    while size <= C:
        half = size // 2
        anchor = jnp.concatenate(
            [jnp.broadcast_to(b_ref[j * size + half - 1:j * size + half, :], (size, W))
             for j in range(tl // size)], axis=0)
        eh = jnp.exp2(-jnp.abs(b_ref[...] - anchor))
        qh = (q_ref[...] * eh).astype(BF16)
        kh = (k_ref[...] * eh).astype(BF16)
        pair_ok = ((row // size) == (col // size)) & ((row % size) >= half) & ((col % size) < half)
        levels.append((qh, kh, pair_ok))
        size *= 2
    lvl = []
    for hl in heads:
        w = jnp.zeros((tl, tl), F32)
        for qh, kh, pair_ok in levels:
            w = jnp.where(pair_ok, _dot_nt(qh[:, hl], kh[:, hl]), w)
        lvl.append(w)

    rc = 64 if tl % 64 == 0 else tl
    r_i = lax.broadcasted_iota(jnp.int32, (rc, tl), 0)
    c_i = lax.broadcasted_iota(jnp.int32, (rc, tl), 1)
    for r0 in range(0, tl, rc):
        lag = jnp.where(((r_i + r0) // Cl) == (c_i // Cl), r_i + r0 - c_i, -1)
        rows = slice(r0, r0 + rc)
        p = [k_ref[rows, hl] for hl in heads]
        a_mat = [jnp.zeros((rc, tl), F32) for _ in heads]
        for d in range(Cl):
            for h, hl in enumerate(heads):
                if d > 0:
                    p[h] = f_ref[rows, hl] * pltpu.roll(p[h], 1, axis=0)
                a = jnp.sum(q_ref[rows, hl] * p[h], axis=-1, keepdims=True)
                a_mat[h] = jnp.where(lag == d, a, a_mat[h])
        for h, hl in enumerate(heads):
            w = (a_mat[h] + lvl[h][r0:r0 + rc, :]).astype(BF16)
            acc = oi_ref[rows, hl] + _dot(w, xi_ref[:, hl].astype(BF16))
            on = acc * lax.rsqrt(jnp.mean(acc * acc, axis=-1, keepdims=True) + NORM_EPS)
            xg = xg_ref[rows, hl]
            o_ref[rows, hl] = (on * ng_ref[:, hl] * (xg * _sigmoid(xg))).astype(o_ref.dtype)


def _hgrn(proj, lb, ng, B, L, col0, tl=128):
    N = proj.shape[0]
    W = lb.shape[1]
    hd = W // HG_HEADS
    tps = L // tl
    col = lambda c: pl.BlockSpec((tl, W), lambda b, t: (b * tps + t, col0 + c))
    return pl.pallas_call(
        _hgrn_kernel,
        out_shape=jax.ShapeDtypeStruct((N, W), BF16),
        grid=(B, tps),
        in_specs=[col(0), col(1), col(2), col(3), _resident((1, W)), _resident((1, W))],
        out_specs=pl.BlockSpec((tl, W), lambda b, t: (b * tps + t, 0)),
        scratch_shapes=[pltpu.VMEM((HG_HEADS, hd, hd), F32),
                        pltpu.VMEM((tl, W), BF16),
                        pltpu.VMEM((tl, W), BF16),
                        pltpu.VMEM((tl, W), F32),
                        pltpu.VMEM((tl, W), F32),
                        pltpu.VMEM((tl, W), F32),
                        pltpu.VMEM((tl, W), F32),
                        pltpu.VMEM((tl, W), F32),
                        pltpu.VMEM((tl, W), F32)],
        compiler_params=_cparams(2),
        name="hgrn2_mixer",
    )(proj, proj, proj, proj, lb, ng)


FFN_HALO = 16


def _ffn_kernel(*refs, n_mix, tiles_per_seq, final_norm, tf):
    x_ref, xhalo_ref = refs[:2]
    mix_refs = refs[2:2 + n_mix]
    mixhalo_refs = refs[2 + n_mix:2 + 2 * n_mix]
    (wp_ref, g_ref, win_ref, cw_ref, cb_ref, wo_ref, gf_ref, o_ref,
     xn_ref, act_ref, a_ref, b_ref, h_ref) = refs[2 + 2 * n_mix:]
    i = pl.program_id(0)
    tm = x_ref.shape[0]
    H = FFN_HALO
    dff = wo_ref.shape[0]

    def mixed(r_ref, m_refs):
        acc = r_ref[...]
        k0 = 0
        for m_ref in m_refs:
            kw = m_ref.shape[1]
            acc = acc + _dot(m_ref[...].astype(BF16), wp_ref[k0:k0 + kw, :])
            k0 += kw
        return acc

    h_ref[...] = mixed(x_ref, mix_refs)
    xn_ref[H:, :] = _rms(h_ref[...], g_ref[...]).astype(BF16)
    hn = _rms(mixed(xhalo_ref, mixhalo_refs), g_ref[...])
    xn_ref[0:H, :] = jnp.where(i % tiles_per_seq == 0, 0.0, hn).astype(BF16)

    def conv(s_ref, c0):
        cols = slice(c0, c0 + tf)
        out = cb_ref[:, cols] + s_ref[H:H + tm, :] * cw_ref[CONV_W - 1:CONV_W, cols]
        for t in range(1, CONV_W):
            out = out + s_ref[H - t:H - t + tm, :] * cw_ref[CONV_W - 1 - t:CONV_W - t, cols]
        return out

    for j in range(dff // tf):
        sa = a_ref.at[j % 2]
        sb = b_ref.at[j % 2]
        sa[...] = _dot(xn_ref[...], win_ref[:, j * tf:(j + 1) * tf])
        sb[...] = _dot(xn_ref[...], win_ref[:, dff + j * tf:dff + (j + 1) * tf])
        a = conv(sa, j * tf)
        b = conv(sb, dff + j * tf)
        act_ref[:, j * tf:(j + 1) * tf] = (a * _sigmoid(a) * b).astype(BF16)

    y = h_ref[...] + _dot(act_ref[...], wo_ref[...])
    if final_norm:
        y = _rms(y, gf_ref[...])
    o_ref[...] = y


def _mix_ffn(x, mix_list, w_proj_bf, layer, g, w_in_bf, conv_w, conv_b, w_out_bf, g_final, L, final_norm,
             tm=512, tf=256):
    N, D = x.shape
    dff = w_out_bf.shape[1]
    H = FFN_HALO
    main = lambda w: pl.BlockSpec((tm, w), lambda i: (i, 0))
    halo = lambda w: pl.BlockSpec((H, w), lambda i: (jnp.maximum(i * (tm // H) - 1, 0), 0))
    kern = functools.partial(_ffn_kernel, n_mix=len(mix_list), tiles_per_seq=L // tm,
                             final_norm=final_norm, tf=tf)
    return pl.pallas_call(
        kern,
        out_shape=jax.ShapeDtypeStruct((N, D), F32),
        grid=(N // tm,),
        in_specs=[main(D), halo(D)]
                 + [main(m.shape[1]) for m in mix_list] + [halo(m.shape[1]) for m in mix_list]
                 + [_resident(w_proj_bf.shape)]
                 + [_resident_layer(a.shape, layer) for a in (g, w_in_bf, conv_w, conv_b, w_out_bf)]
                 + [_resident((1, D))],
        out_specs=pl.BlockSpec((tm, D), lambda i: (i, 0)),
        scratch_shapes=[pltpu.VMEM((tm + H, D), BF16),
                        pltpu.VMEM((tm, dff), BF16),
                        pltpu.VMEM((2, tm + H, tf), F32),
                        pltpu.VMEM((2, tm + H, tf), F32),
                        pltpu.VMEM((tm, D), F32)],
        compiler_params=_cparams(1),
        name="mix_conv_ffn",
    )(x, x, *mix_list, *mix_list, w_proj_bf, g, w_in_bf, conv_w, conv_b, w_out_bf, g_final)


def _rot_lane_layout():
    half = ROT_DIM // 2
    E = ATT_HEAD_DIM
    lay = []
    for base in (0, half):
        lay += [(0, base + i) for i in range(half)] + [(1, base + i) for i in range(half)]
        lay += [(base // half, d) for d in range(ROT_DIM, E)]
    return lay


def _attn_kernel(q_ref, k_ref, v_ref, o_ref, m_a_ref, m_b_ref, acc_a_ref, acc_b_ref, tmp_ref, stage_ref,
                 qs_ref, ka_ref, kb_ref, va_ref, vb_ref):
    grp = pl.program_id(2)
    L, lw = q_ref.shape
    T = ATT_BLOCK
    E = ATT_HEAD_DIM
    n_groups = len(ATT_DILATIONS)

    lane1 = lax.broadcasted_iota(jnp.int32, (1, lw), 1)
    head_of_lane = np.array([h for h, _ in _rot_lane_layout()])
    in_a_qk = jnp.zeros((1, lw), jnp.bool_)
    for lo, hi in ((0, 8), (16, 72)):
        in_a_qk = in_a_qk | ((lane1 >= lo) & (lane1 < hi))
    assert all((head_of_lane[l] == 0) == (l < 8 or 16 <= l < 72) for l in range(lw))
    in_a_v = lane1 < E
    qi = lax.broadcasted_iota(jnp.int32, (T, T), 0)
    kj = lax.broadcasted_iota(jnp.int32, (T, T), 1)
    cur_ok = kj <= qi
    both_ok = jnp.concatenate([kj >= qi, cur_ok], axis=1)

    S = ATT_DILATIONS[1]
    assert ATT_DILATIONS == (1, S, S * S) and T % S == 0
    Ls = L // S

    def by_s_rows(dil, r, n):
        if dil == S:
            return slice(r * Ls + n * T, r * Ls + (n + 1) * T)
        return pl.ds((r % S) * Ls + r // S, T, stride=S)

    def run_group(gi, dil):
        M = L // dil
        nblk = M // T

        def residue_major(x_ref, emit):
            if dil == S * S:
                for r4 in range(S):
                    tmp_ref[r4 * Ls:(r4 + 1) * Ls, :] = x_ref[pl.ds(r4, Ls, stride=S), :]
            for r in range(dil):
                if dil == 1:
                    xr = x_ref[...]
                elif dil == S:
                    xr = x_ref[pl.ds(r, M, stride=S), :]
                else:
                    xr = tmp_ref[by_s_rows(dil, r, 0), :]
                emit(slice(r * M, (r + 1) * M), xr)

        def emit_q(dst, xr):
            qs_ref[dst, :] = (xr * (E ** -0.5 * LOG2E)).astype(BF16)

        def emit_k(dst, xr):
            ka_ref[dst, :] = jnp.where(in_a_qk, xr, 0.0).astype(BF16)
            kb_ref[dst, :] = jnp.where(in_a_qk, 0.0, xr).astype(BF16)

        def emit_v(dst, xr):
            va_ref[dst, :] = jnp.where(in_a_v, xr, 1.0).astype(BF16)
            vb_ref[dst, :] = jnp.where(in_a_v, 1.0, xr).astype(BF16)

        residue_major(q_ref, emit_q)
        residue_major(k_ref, emit_k)
        residue_major(v_ref, emit_v)

        blocks = [(r, n) for r in range(dil) for n in range(nblk)]

        def load_state(ref, r, n):
            return ref[by_s_rows(dil, r, n), :]

        def store_state(ref, which, r, n, val):
            if dil == 1:
                stage_ref[which] = val
                for j in range(S):
                    ref[j * Ls + (T // S) * n:j * Ls + (T // S) * (n + 1), :] = (
                        stage_ref[which, pl.ds(j, T // S, stride=S), :])
            else:
                ref[by_s_rows(dil, r, n), :] = val

        def kv_rows(r, n):
            c0 = r * M + n * T
            return slice(c0 - T, c0 + T) if n > 0 else slice(c0, c0 + T)

        def scores(r, n):
            c0 = r * M + n * T
            q = qs_ref[c0:c0 + T, :]
            return _dot_nt(q, ka_ref[kv_rows(r, n), :]), _dot_nt(q, kb_ref[kv_rows(r, n), :])

        def softmax(r, n, s_a, s_b):
            ok = both_ok if n > 0 else cur_ok
            s_a = jnp.where(ok, s_a, NEG_BIG)
            s_b = jnp.where(ok, s_b, NEG_BIG)
            m_a = jnp.broadcast_to(jnp.max(s_a, axis=-1, keepdims=True), (T, lw))
            m_b = jnp.broadcast_to(jnp.max(s_b, axis=-1, keepdims=True), (T, lw))
            al_a = al_b = None
            if gi > 0:
                m_old_a = load_state(m_a_ref, r, n)
                m_old_b = load_state(m_b_ref, r, n)
                m_a = jnp.maximum(m_a, m_old_a)
                m_b = jnp.maximum(m_b, m_old_b)
                al_a = jnp.exp2(m_old_a - m_a)
                al_b = jnp.exp2(m_old_b - m_b)
            if gi < n_groups - 1:
                store_state(m_a_ref, 0, r, n, m_a)
                store_state(m_b_ref, 3, r, n, m_b)
            if n > 0:
                m_a = jnp.concatenate([m_a, m_a], axis=1)
                m_b = jnp.concatenate([m_b, m_b], axis=1)
            return jnp.exp2(s_a - m_a).astype(BF16), jnp.exp2(s_b - m_b).astype(BF16), al_a, al_b

        def values(r, n, p_a, p_b, al_a, al_b):
            pv_a = _dot(p_a, va_ref[kv_rows(r, n), :])
            pv_b = _dot(p_b, vb_ref[kv_rows(r, n), :])
            if gi > 0:
                pv_a = al_a * load_state(acc_a_ref, r, n) + pv_a
                pv_b = al_b * load_state(acc_b_ref, r, n) + pv_b
            if gi < n_groups - 1:
                store_state(acc_a_ref, 1, r, n, pv_a)
                store_state(acc_b_ref, 2, r, n, pv_b)
            else:
                tmp_ref[by_s_rows(dil, r, n), :] = jnp.where(
                    in_a_v, pv_a / pltpu.roll(pv_a, E, axis=1), pv_b / pltpu.roll(pv_b, E, axis=1))

        s_val, p_val = {}, {}
        for step in range(len(blocks) + 2):
            if step < len(blocks):
                s_val[step] = scores(*blocks[step])
            if 0 <= step - 1 < len(blocks):
                p_val[step - 1] = softmax(*blocks[step - 1], *s_val.pop(step - 1))
            if 0 <= step - 2 < len(blocks):
                values(*blocks[step - 2], *p_val.pop(step - 2))

        if gi == n_groups - 1:
            for r4 in range(S):
                o_ref[pl.ds(r4, Ls, stride=S), :] = tmp_ref[r4 * Ls:(r4 + 1) * Ls, :]

    for gi, dil in enumerate(ATT_DILATIONS):
        pl.when(grp == gi)(functools.partial(run_group, gi, dil))


def _attn(qkv, B, L, gw):
    N = qkv.shape[0]
    lw = 2 * ATT_HEAD_DIM
    npair = gw // lw
    ng = len(ATT_DILATIONS)

    def spec(kind):
        return pl.BlockSpec((L, lw), lambda b, p, g: (b, kind * ng * npair + g * npair + p))

    return pl.pallas_call(
        _attn_kernel,
        out_shape=jax.ShapeDtypeStruct((N, gw), F32),
        grid=(B, npair, ng),
        in_specs=[spec(0), spec(1), spec(2)],
        out_specs=pl.BlockSpec((L, lw), lambda b, p, g: (b, p)),
        scratch_shapes=[pltpu.VMEM((L, lw), F32)] * 4
                       + [pltpu.VMEM((L, lw), F32),
                          pltpu.VMEM((4, ATT_BLOCK, lw), F32)]
                       + [pltpu.VMEM((L, lw), BF16)] * 5,
        compiler_params=_cparams(3),
        name="dilated_attention",
    )(qkv, qkv, qkv)


def _s5_params(A_re, A_im, log_dt, B_re, B_im, C_re, C_im, Dd):
    G, P = A_re.shape
    dt = jnp.exp(log_dt)[:, None]
    mag = jnp.exp(A_re * dt)
    ab_re = mag * jnp.cos(A_im * dt)
    ab_im = mag * jnp.sin(A_im * dt)
    den = A_re * A_re + A_im * A_im
    nr, ni = ab_re - 1.0, ab_im
    c_re = (nr * A_re + ni * A_im) / den
    c_im = (ni * A_re - nr * A_im) / den
    Bb_re = c_re[..., None] * B_re - c_im[..., None] * B_im
    Bb_im = c_re[..., None] * B_im + c_im[..., None] * B_re
    hg = G // 2
    eye = jnp.eye(hg, dtype=F32)

    def bd_in(m):
        return jnp.einsum('gpc,gh->gchp', m, eye).reshape(hg * S5_GROUP, hg * P)

    def bd_out(m):
        return jnp.einsum('gcp,gh->gphc', m, eye).reshape(hg * P, hg * S5_GROUP)

    bds, cds = [], []
    for k in range(2):
        sl = slice(k * hg, (k + 1) * hg)
        bds.append(jnp.concatenate([bd_in(Bb_re[sl]), bd_in(Bb_im[sl])], axis=1).astype(BF16))
        cds.append(jnp.concatenate([bd_out(C_re[sl]), -bd_out(C_im[sl])], axis=0).astype(BF16))
    return bds[0], bds[1], ab_re.reshape(1, G * P), ab_im.reshape(1, G * P), cds[0], cds[1], Dd.reshape(1, -1)


ROPE_COLS = 32


def _rope_tables(positions):
    half = ROT_DIM // 2
    inv_freq = ROPE_THETA ** (-jnp.arange(half, dtype=F32) * 2.0 / ROT_DIM)
    ang = positions.astype(F32).reshape(-1, 1) * inv_freq
    n = ang.shape[0]
    cs = jnp.concatenate([jnp.cos(ang), jnp.sin(ang), jnp.ones((n, 1), F32),
                          jnp.zeros((n, ROPE_COLS - 2 * half - 1), F32)], axis=1)
    ex = np.zeros((ROPE_COLS, 2 * LANES), np.float32)
    for lane, (_, d) in enumerate(_rot_lane_layout()):
        if d < half:
            ex[d, lane] = 1.0
            ex[half + d, LANES + lane] = -1.0
        elif d < ROT_DIM:
            ex[d - half, lane] = 1.0
            ex[half + d - half, LANES + lane] = 1.0
        else:
            ex[2 * half, lane] = 1.0
    return cs, jnp.asarray(ex, dtype=BF16)


def _qk_lane_layout(w):
    rows, n_cols = w.shape
    E = ATT_HEAD_DIM
    wp = w.reshape(rows, n_cols // LANES, 2, E)
    lay = _rot_lane_layout()
    pieces, start = [], 0
    for l in range(1, LANES + 1):
        if l == LANES or lay[l][0] != lay[start][0] or lay[l][1] != lay[l - 1][1] + 1:
            h, d0 = lay[start]
            pieces.append(wp[:, :, h, d0:d0 + (l - start)])
            start = l
    return jnp.concatenate(pieces, axis=-1).reshape(rows, n_cols)


def kernel(x, positions, norm_mix, norm_ffn, norm_final, mix_w_in, mix_w_out, s5_A_re, s5_A_im, s5_log_dt, s5_B_re, s5_B_im, s5_C_re, s5_C_im, s5_D, s5_glu_w, s5_glu_b, hgrn_gamma, hgrn_norm, att_w_qkv, att_w_o, ffn_w_in, ffn_conv_w, ffn_conv_b, ffn_w_out):
    B, L, D = x.shape
    N = B * L
    x2 = x.reshape(N, D)
    tm = 512
    s5w = s5_A_re.shape[1] * S5_GROUP
    hgw = hgrn_norm.shape[1]

    proj = _norm_proj(x2, norm_mix[0:1], mix_w_in[0].astype(BF16), tm, s5w, name="inproj")
    s5p = _s5_params(s5_A_re[0], s5_A_im[0], s5_log_dt[0], s5_B_re[0], s5_B_im[0],
                     s5_C_re[0], s5_C_im[0], s5_D[0])
    oa = _s5(proj.reshape(B, L, -1), *s5p, s5_glu_w[0].astype(BF16), s5_glu_b[0:1], s5w)
    lb_all = jnp.cumsum(jax.nn.softmax(hgrn_gamma.astype(F32), axis=0), axis=0)
    ob = _hgrn(proj, lb_all[0:1], hgrn_norm[0:1], B, L, col0=s5w // hgw)
    ffn_args = (norm_ffn[:, None, :], ffn_w_in.astype(BF16), ffn_conv_w, ffn_conv_b[:, None, :],
                ffn_w_out.astype(BF16), norm_final.reshape(1, D), L)
    h = _mix_ffn(x2, [oa.reshape(N, s5w), ob], mix_w_out[0].astype(BF16), 0, *ffn_args, final_norm=False)

    gw = att_w_o.shape[1]
    n_rot_cols = 2 * len(ATT_DILATIONS) * gw
    w_qkv = att_w_qkv[0].astype(BF16)
    w_qkv = jnp.concatenate([_qk_lane_layout(w_qkv[:, :n_rot_cols]), w_qkv[:, n_rot_cols:]], axis=1)
    qkv = _norm_proj(h, norm_mix[1:2], w_qkv, tm, gw, rot=_rope_tables(positions),
                     n_rot_cols=n_rot_cols, name="qkv_rotary")
    ao = _attn(qkv, B, L, gw)
    h = _mix_ffn(h, [ao], att_w_o[0].astype(BF16), 1, *ffn_args, final_norm=True)
    return h.reshape(B, L, D)
```

```python
import functools
import math

import numpy as np
import jax
import jax.numpy as jnp
from jax import lax
from jax.experimental import pallas as pl
from jax.experimental.pallas import tpu as pltpu

F32 = jnp.float32
BF16 = jnp.bfloat16

NORM_EPS = 1e-6
S5_GROUP = 16
S5_STATE = 64
HG_HEADS = 4
HG_BLOCK = 128
HG_LAG_BLOCK = 4
ATT_HEAD_DIM = 64
ATT_BLOCK = 128
ATT_DILATIONS = (1, 4, 16)
ROT_DIM = 16
ROPE_THETA = 500000.0
CONV_W = 3
NEG_BIG = -1e30
LOG2E = math.log2(math.e)
LANES = 128

VMEM_LIMIT = 56 * 1024 * 1024


def _cparams(n_axes):
    return pltpu.CompilerParams(dimension_semantics=("arbitrary",) * n_axes,
                                vmem_limit_bytes=VMEM_LIMIT)


def _resident(shape):
    return pl.BlockSpec(shape, lambda *_: (0,) * len(shape), pipeline_mode=pl.Buffered(1))


def _resident_layer(shape, layer):
    return pl.BlockSpec((None,) + tuple(shape[1:]), lambda *_: (layer,) + (0,) * (len(shape) - 1),
                        pipeline_mode=pl.Buffered(1))


def _rms(x, g):
    return x * lax.rsqrt(jnp.mean(x * x, axis=-1, keepdims=True) + NORM_EPS) * g


def _sigmoid(x):
    return 1.0 / (1.0 + jnp.exp(-x))


def _dot(a, b):
    return jnp.dot(a, b, preferred_element_type=F32)


def _dot_nt(a, b):
    return lax.dot_general(a, b, (((1,), (1,)), ((), ())), preferred_element_type=F32)


def _dot_tn(a, b):
    return lax.dot_general(a, b, (((0,), (0,)), ((), ())), preferred_element_type=F32)


def _norm_proj_kernel(x_ref, g_ref, w_ref, *rest, tn, n_rot_tiles):
    if n_rot_tiles:
        cs_ref, ex_ref, o_ref, xn_ref = rest
        cs = cs_ref[...]
        cs_hi = cs.astype(BF16)
        cs_lo = (cs - cs_hi.astype(F32)).astype(BF16)
        tab = _dot(cs_hi, ex_ref[...]) + _dot(cs_lo, ex_ref[...])
        c = tab[:, :LANES]
        s = tab[:, LANES:]
    else:
        o_ref, xn_ref = rest
    xn_ref[...] = _rms(x_ref[...], g_ref[...]).astype(BF16)
    for j in range(w_ref.shape[1] // tn):
        y = _dot(xn_ref[...], w_ref[:, j * tn:(j + 1) * tn])
        if j < n_rot_tiles:
            for k in range(tn // LANES):
                t = y[:, k * LANES:(k + 1) * LANES]
                o_ref[:, j * tn + k * LANES:j * tn + (k + 1) * LANES] = (
                    t * c + pltpu.roll(t, LANES // 2, axis=1) * s)
        else:
            o_ref[:, j * tn:(j + 1) * tn] = y


def _norm_proj(x2, g, w_bf, tm, tn, rot=None, n_rot_cols=0, name="norm_proj"):
    N, D = x2.shape
    ncols = w_bf.shape[1]
    in_specs = [pl.BlockSpec((tm, D), lambda i: (i, 0)), _resident((1, D)), _resident(w_bf.shape)]
    args = [x2, g, w_bf]
    if rot is not None:
        cs, expand = rot
        in_specs += [pl.BlockSpec((tm, cs.shape[1]), lambda i: (i, 0)), _resident(expand.shape)]
        args += [cs, expand]
    return pl.pallas_call(
        functools.partial(_norm_proj_kernel, tn=tn, n_rot_tiles=n_rot_cols // tn),
        out_shape=jax.ShapeDtypeStruct((N, ncols), F32),
        grid=(N // tm,),
        in_specs=in_specs,
        out_specs=pl.BlockSpec((tm, ncols), lambda i: (i, 0)),
        scratch_shapes=[pltpu.VMEM((tm, D), BF16)],
        compiler_params=_cparams(1),
        name=name,
    )(*args)


def _s5_kernel(u_ref, bd0_ref, bd1_ref, ar_ref, ai_ref, c0_ref, c1_ref, d_ref, gw_ref, gb_ref,
               o_ref, x0_ref, x1_ref, st_ref, pf_ref, pb_ref, *, lane_chunk):
    nb, ts, W = u_ref.shape
    tm = nb * ts
    half = x0_ref.shape[1] // 2

    @pl.when(pl.program_id(0) == 0)
    def _():
        st_ref[...] = jnp.zeros_like(st_ref)
        r = lax.broadcasted_iota(jnp.int32, (tm, tm), 0)
        c = lax.broadcasted_iota(jnp.int32, (tm, tm), 1)
        pf_ref[...] = jnp.where(c == (r % nb) * ts + r // nb, 1.0, 0.0).astype(BF16)
        pb_ref[...] = jnp.where(r == (c % nb) * ts + c // nb, 1.0, 0.0).astype(BF16)

    u_bm = u_ref[...].reshape(tm, W)
    u_hi = u_bm.astype(BF16)
    u_lo = (u_bm - u_hi.astype(F32)).astype(BF16)
    ut_hi = _dot(pf_ref[...], u_hi)
    u = ut_hi + _dot(pf_ref[...], u_lo)
    ub = ut_hi.astype(BF16)
    ku = W // 2
    x0_ref[...] = _dot(ub[:, :ku], bd0_ref[...])
    x1_ref[...] = _dot(ub[:, ku:], bd1_ref[...])

    def scan(k, x_ref):
        for c in range(half // lane_chunk):
            lo = c * lane_chunk
            re = slice(lo, lo + lane_chunk)
            im = slice(half + lo, half + lo + lane_chunk)
            a_re = jnp.broadcast_to(ar_ref[:, k * half + lo:k * half + lo + lane_chunk], (nb, lane_chunk))
            a_im = jnp.broadcast_to(ai_ref[:, k * half + lo:k * half + lo + lane_chunk], (nb, lane_chunk))
            xr, xi = st_ref[2 * k, :, re], st_ref[2 * k + 1, :, re]
            for s in range(ts):
                rows = slice(s * nb, (s + 1) * nb)
                xr, xi = (a_re * xr - a_im * xi + x_ref[rows, re],
                          a_re * xi + a_im * xr + x_ref[rows, im])
                x_ref[rows, re] = xr
                x_ref[rows, im] = xi
            st_ref[2 * k, :, re] = xr
            st_ref[2 * k + 1, :, re] = xi

    scan(0, x0_ref)
    y0 = _dot(x0_ref[...].astype(BF16), c0_ref[...])
    scan(1, x1_ref)
    y1 = _dot(x1_ref[...].astype(BF16), c1_ref[...])
    y = jnp.concatenate([y0, y1], axis=1) + d_ref[...] * u
    z = 0.5 * y * (1.0 + jnp.tanh(math.sqrt(2.0 / math.pi) * (y + 0.044715 * (y * y * y))))
    gate = _sigmoid(_dot(z.astype(BF16), gw_ref[...]) + gb_ref[...])
    o_tm = (z * gate).astype(BF16)
    o_ref[...] = _dot(pb_ref[...], o_tm).astype(BF16).reshape(nb, ts, W)


def _s5(proj3, bd0, bd1, ar, ai, c0, c1, dd, gw, gb, W, ts=64):
    B, L, _ = proj3.shape
    S2 = bd0.shape[1]
    tm = B * ts
    kern = functools.partial(_s5_kernel, lane_chunk=1024)
    return pl.pallas_call(
        kern,
        out_shape=jax.ShapeDtypeStruct((B, L, W), BF16),
        grid=(L // ts,),
        in_specs=[pl.BlockSpec((B, ts, W), lambda t: (0, t, 0))]
                 + [_resident(a.shape) for a in (bd0, bd1, ar, ai, c0, c1, dd, gw, gb)],
        out_specs=pl.BlockSpec((B, ts, W), lambda t: (0, t, 0)),
        scratch_shapes=[pltpu.VMEM((tm, S2), F32), pltpu.VMEM((tm, S2), F32),
                        pltpu.VMEM((4, B, S2 // 2), F32),
                        pltpu.VMEM((tm, tm), BF16), pltpu.VMEM((tm, tm), BF16)],
        compiler_params=_cparams(1),
        name="s5_mixer",
    )(proj3, bd0, bd1, ar, ai, c0, c1, dd, gw, gb)


def _hgrn_kernel(xq_ref, xf_ref, xi_ref, xg_ref, lb_ref, ng_ref, o_ref,
                 st_ref, qe_ref, ke_ref, eb_ref, q_ref, k_ref, f_ref, b_ref, oi_ref):
    tl, W = xq_ref.shape
    C = HG_BLOCK
    hd = W // HG_HEADS
    nblk = tl // C

    @pl.when(pl.program_id(1) == 0)
    def _():
        st_ref[...] = jnp.zeros_like(st_ref)

    xq = xq_ref[...]
    q = xq * _sigmoid(xq)
    lb = lb_ref[...]
    f = lb + (1.0 - lb) * _sigmoid(xf_ref[...])
    kk = 1.0 - f
    g2 = jnp.log(f) * LOG2E

    row = lax.broadcasted_iota(jnp.int32, (tl, tl), 0)
    col = lax.broadcasted_iota(jnp.int32, (tl, tl), 1)
    same = (row // C) == (col // C)
    t_cum = jnp.where(same & (col <= row), 1.0, 0.0).astype(BF16)
    t_blk = jnp.where(same, 1.0, 0.0).astype(BF16)
    g_hi = g2.astype(BF16)
    g_lo = (g2 - g_hi.astype(F32)).astype(BF16)
    b = _dot(t_cum, g_hi) + _dot(t_cum, g_lo)
    bl = _dot(t_blk, g_hi) + _dot(t_blk, g_lo)

    q_ref[...] = q
    k_ref[...] = kk
    f_ref[...] = f
    b_ref[...] = b
    qe_ref[...] = (q * jnp.exp2(b)).astype(BF16)
    ke_ref[...] = (kk * jnp.exp2(bl - b)).astype(BF16)
    eb_ref[...] = jnp.exp2(bl)

    heads = [slice(h * hd, (h + 1) * hd) for h in range(HG_HEADS)]
    st = [st_ref[h] for h in range(HG_HEADS)]
    for i in range(nblk):
        rows = slice(i * C, (i + 1) * C)
        for h, hl in enumerate(heads):
            oi_ref[rows, hl] = _dot_nt(qe_ref[rows, hl], st[h].astype(BF16))
            vb = xi_ref[rows, hl].astype(BF16)
            st[h] = st[h] * eb_ref[i * C:i * C + 1, hl] + _dot_tn(vb, ke_ref[rows, hl])
    for h in range(HG_HEADS):
        st_ref[h] = st[h]

    Cl = HG_LAG_BLOCK
    levels = []
    size = 2 * Cl
    while size <= C:
        half = size // 2
        anchor = jnp.concatenate(
            [jnp.broadcast_to(b_ref[j * size + half - 1:j * size + half, :], (size, W))
             for j in range(tl // size)], axis=0)
        eh = jnp.exp2(-jnp.abs(b_ref[...] - anchor))
        qh = (q_ref[...] * eh).astype(BF16)
        kh = (k_ref[...] * eh).astype(BF16)
        pair_ok = ((row // size) == (col // size)) & ((row % size) >= half) & ((col % size) < half)
        levels.append((qh, kh, pair_ok))
        size *= 2
    lvl = []
    for hl in heads:
        w = jnp.zeros((tl, tl), F32)
        for qh, kh, pair_ok in levels:
            w = jnp.where(pair_ok, _dot_nt(qh[:, hl], kh[:, hl]), w)
        lvl.append(w)

    rc = 64 if tl % 64 == 0 else tl
    r_i = lax.broadcasted_iota(jnp.int32, (rc, tl), 0)
    c_i = lax.broadcasted_iota(jnp.int32, (rc, tl), 1)
    for r0 in range(0, tl, rc):
        lag = jnp.where(((r_i + r0) // Cl) == (c_i // Cl), r_i + r0 - c_i, -1)
        rows = slice(r0, r0 + rc)
        p = [k_ref[rows, hl] for hl in heads]
        a_mat = [jnp.zeros((rc, tl), F32) for _ in heads]
        for d in range(Cl):
            for h, hl in enumerate(heads):
                if d > 0:
                    p[h] = f_ref[rows, hl] * pltpu.roll(p[h], 1, axis=0)
                a = jnp.sum(q_ref[rows, hl] * p[h], axis=-1, keepdims=True)
                a_mat[h] = jnp.where(lag == d, a, a_mat[h])
        for h, hl in enumerate(heads):
            w = (a_mat[h] + lvl[h][r0:r0 + rc, :]).astype(BF16)
            acc = oi_ref[rows, hl] + _dot(w, xi_ref[:, hl].astype(BF16))
            on = acc * lax.rsqrt(jnp.mean(acc * acc, axis=-1, keepdims=True) + NORM_EPS)
            xg = xg_ref[rows, hl]
            o_ref[rows, hl] = (on * ng_ref[:, hl] * (xg * _sigmoid(xg))).astype(o_ref.dtype)


def _hgrn(proj, lb, ng, B, L, col0, tl=128):
    N = proj.shape[0]
    W = lb.shape[1]
    hd = W // HG_HEADS
    tps = L // tl
    col = lambda c: pl.BlockSpec((tl, W), lambda b, t: (b * tps + t, col0 + c))
    return pl.pallas_call(
        _hgrn_kernel,
        out_shape=jax.ShapeDtypeStruct((N, W), BF16),
        grid=(B, tps),
        in_specs=[col(0), col(1), col(2), col(3), _resident((1, W)), _resident((1, W))],
        out_specs=pl.BlockSpec((tl, W), lambda b, t: (b * tps + t, 0)),
        scratch_shapes=[pltpu.VMEM((HG_HEADS, hd, hd), F32),
                        pltpu.VMEM((tl, W), BF16),
                        pltpu.VMEM((tl, W), BF16),
                        pltpu.VMEM((tl, W), F32),
                        pltpu.VMEM((tl, W), F32),
                        pltpu.VMEM((tl, W), F32),
                        pltpu.VMEM((tl, W), F32),
                        pltpu.VMEM((tl, W), F32),
                        pltpu.VMEM((tl, W), F32)],
        compiler_params=_cparams(2),
        name="hgrn2_mixer",
    )(proj, proj, proj, proj, lb, ng)


FFN_HALO = 16


def _ffn_kernel(*refs, n_mix, tiles_per_seq, final_norm, tf):
    x_ref, xhalo_ref = refs[:2]
    mix_refs = refs[2:2 + n_mix]
    mixhalo_refs = refs[2 + n_mix:2 + 2 * n_mix]
    (wp_ref, g_ref, win_ref, cw_ref, cb_ref, wo_ref, gf_ref, o_ref,
     xn_ref, act_ref, a_ref, b_ref, h_ref) = refs[2 + 2 * n_mix:]
    i = pl.program_id(0)
    tm = x_ref.shape[0]
    H = FFN_HALO
    dff = wo_ref.shape[0]

    def mixed(r_ref, m_refs, rows):
        acc = r_ref[rows, :]
        k0 = 0
        for m_ref in m_refs:
            kw = m_ref.shape[1]
            acc = acc + _dot(m_ref[rows, :].astype(BF16), wp_ref[k0:k0 + kw, :])
            k0 += kw
        return acc

    hm = tm // 2
    parts = [(0, hm), (hm, tm)]
    hn = _rms(mixed(xhalo_ref, mixhalo_refs, slice(0, H)), g_ref[...])
    xn_ref[0:H, :] = jnp.where(i % tiles_per_seq == 0, 0.0, hn).astype(BF16)
    for lo, hi in parts:
        h_ref[lo:hi, :] = mixed(x_ref, mix_refs, slice(lo, hi))
        xn_ref[H + lo:H + hi, :] = _rms(h_ref[lo:hi, :], g_ref[...]).astype(BF16)

    def conv(s_ref, c0, lo, hi):
        cols = slice(c0, c0 + tf)
        out = cb_ref[:, cols] + s_ref[H + lo:H + hi, :] * cw_ref[CONV_W - 1:CONV_W, cols]
        for t in range(1, CONV_W):
            out = out + s_ref[H + lo - t:H + hi - t, :] * cw_ref[CONV_W - 1 - t:CONV_W - t, cols]
        return out

    for j in range(dff // tf):
        sa = a_ref.at[j % 2]
        sb = b_ref.at[j % 2]
        for lo, hi in parts:
            strip = slice(0 if lo == 0 else H + lo, H + hi)
            sa[strip, :] = _dot(xn_ref[strip, :], win_ref[:, j * tf:(j + 1) * tf])
            sb[strip, :] = _dot(xn_ref[strip, :], win_ref[:, dff + j * tf:dff + (j + 1) * tf])
        for lo, hi in parts:
            a = conv(sa, j * tf, lo, hi)
            b = conv(sb, dff + j * tf, lo, hi)
            act_ref[lo:hi, j * tf:(j + 1) * tf] = (a * _sigmoid(a) * b).astype(BF16)

    for lo, hi in parts:
        y = h_ref[lo:hi, :] + _dot(act_ref[lo:hi, :], wo_ref[...])
        if final_norm:
            y = _rms(y, gf_ref[...])
        o_ref[lo:hi, :] = y


def _mix_ffn(x, mix_list, w_proj_bf, layer, g, w_in_bf, conv_w, conv_b, w_out_bf, g_final, L, final_norm,
             tm=512, tf=256):
    N, D = x.shape
    dff = w_out_bf.shape[1]
    H = FFN_HALO
    main = lambda w: pl.BlockSpec((tm, w), lambda i: (i, 0))
    halo = lambda w: pl.BlockSpec((H, w), lambda i: (jnp.maximum(i * (tm // H) - 1, 0), 0))
    kern = functools.partial(_ffn_kernel, n_mix=len(mix_list), tiles_per_seq=L // tm,
                             final_norm=final_norm, tf=tf)
    return pl.pallas_call(
        kern,
        out_shape=jax.ShapeDtypeStruct((N, D), F32),
        grid=(N // tm,),
        in_specs=[main(D), halo(D)]
                 + [main(m.shape[1]) for m in mix_list] + [halo(m.shape[1]) for m in mix_list]
                 + [_resident(w_proj_bf.shape)]
                 + [_resident_layer(a.shape, layer) for a in (g, w_in_bf, conv_w, conv_b, w_out_bf)]
                 + [_resident((1, D))],
        out_specs=pl.BlockSpec((tm, D), lambda i: (i, 0)),
        scratch_shapes=[pltpu.VMEM((tm + H, D), BF16),
                        pltpu.VMEM((tm, dff), BF16),
                        pltpu.VMEM((2, tm + H, tf), F32),
                        pltpu.VMEM((2, tm + H, tf), F32),
                        pltpu.VMEM((tm, D), F32)],
        compiler_params=_cparams(1),
        name="mix_conv_ffn",
    )(x, x, *mix_list, *mix_list, w_proj_bf, g, w_in_bf, conv_w, conv_b, w_out_bf, g_final)


def _rot_lane_layout():
    half = ROT_DIM // 2
    E = ATT_HEAD_DIM
    lay = []
    for base in (0, half):
        lay += [(0, base + i) for i in range(half)] + [(1, base + i) for i in range(half)]
        lay += [(base // half, d) for d in range(ROT_DIM, E)]
    return lay


def _attn_kernel(q_ref, k_ref, v_ref, o_ref, m_a_ref, m_b_ref, acc_a_ref, acc_b_ref, tmp_ref, stage_ref,
                 qs_ref, ka_ref, kb_ref, va_ref, vb_ref):
    grp = pl.program_id(2)
    L, lw = q_ref.shape
    T = ATT_BLOCK
    E = ATT_HEAD_DIM
    n_groups = len(ATT_DILATIONS)

    lane1 = lax.broadcasted_iota(jnp.int32, (1, lw), 1)
    head_of_lane = np.array([h for h, _ in _rot_lane_layout()])
    in_a_qk = jnp.zeros((1, lw), jnp.bool_)
    for lo, hi in ((0, 8), (16, 72)):
        in_a_qk = in_a_qk | ((lane1 >= lo) & (lane1 < hi))
    assert all((head_of_lane[l] == 0) == (l < 8 or 16 <= l < 72) for l in range(lw))
    in_a_v = lane1 < E
    qi = lax.broadcasted_iota(jnp.int32, (T, T), 0)
    kj = lax.broadcasted_iota(jnp.int32, (T, T), 1)
    cur_ok = kj <= qi
    both_ok = jnp.concatenate([kj >= qi, cur_ok], axis=1)

    S = ATT_DILATIONS[1]
    assert ATT_DILATIONS == (1, S, S * S) and T % S == 0
    Ls = L // S

    def by_s_rows(dil, r, n):
        if dil == S:
            return slice(r * Ls + n * T, r * Ls + (n + 1) * T)
        return pl.ds((r % S) * Ls + r // S, T, stride=S)

    def run_group(gi, dil):
        M = L // dil
        nblk = M // T

        def residue_major(x_ref, emit):
            if dil == S * S:
                for r4 in range(S):
                    tmp_ref[r4 * Ls:(r4 + 1) * Ls, :] = x_ref[pl.ds(r4, Ls, stride=S), :]
            for r in range(dil):
                if dil == 1:
                    xr = x_ref[...]
                elif dil == S:
                    xr = x_ref[pl.ds(r, M, stride=S), :]
                else:
                    xr = tmp_ref[by_s_rows(dil, r, 0), :]
                emit(slice(r * M, (r + 1) * M), xr)

        def emit_q(dst, xr):
            qs_ref[dst, :] = (xr * (E ** -0.5 * LOG2E)).astype(BF16)

        def emit_k(dst, xr):
            ka_ref[dst, :] = jnp.where(in_a_qk, xr, 0.0).astype(BF16)
            kb_ref[dst, :] = jnp.where(in_a_qk, 0.0, xr).astype(BF16)

        def emit_v(dst, xr):
            va_ref[dst, :] = jnp.where(in_a_v, xr, 1.0).astype(BF16)
            vb_ref[dst, :] = jnp.where(in_a_v, 1.0, xr).astype(BF16)

        residue_major(q_ref, emit_q)
        residue_major(k_ref, emit_k)
        residue_major(v_ref, emit_v)

        blocks = [(r, n) for r in range(dil) for n in range(nblk)]

        def load_state(ref, r, n):
            return ref[by_s_rows(dil, r, n), :]

        def store_state(ref, which, r, n, val):
            if dil == 1:
                stage_ref[which] = val
                for j in range(S):
                    ref[j * Ls + (T // S) * n:j * Ls + (T // S) * (n + 1), :] = (
                        stage_ref[which, pl.ds(j, T // S, stride=S), :])
            else:
                ref[by_s_rows(dil, r, n), :] = val

        def kv_rows(r, n):
            c0 = r * M + n * T
            return slice(c0 - T, c0 + T) if n > 0 else slice(c0, c0 + T)

        def scores(r, n):
            c0 = r * M + n * T
            q = qs_ref[c0:c0 + T, :]
            return _dot_nt(q, ka_ref[kv_rows(r, n), :]), _dot_nt(q, kb_ref[kv_rows(r, n), :])

        def softmax(r, n, s_a, s_b):
            ok = both_ok if n > 0 else cur_ok
            s_a = jnp.where(ok, s_a, NEG_BIG)
            s_b = jnp.where(ok, s_b, NEG_BIG)
            m_a = jnp.broadcast_to(jnp.max(s_a, axis=-1, keepdims=True), (T, lw))
            m_b = jnp.broadcast_to(jnp.max(s_b, axis=-1, keepdims=True), (T, lw))
            al_a = al_b = None
            if gi > 0:
                m_old_a = load_state(m_a_ref, r, n)
                m_old_b = load_state(m_b_ref, r, n)
                m_a = jnp.maximum(m_a, m_old_a)
                m_b = jnp.maximum(m_b, m_old_b)
                al_a = jnp.exp2(m_old_a - m_a)
                al_b = jnp.exp2(m_old_b - m_b)
            if gi < n_groups - 1:
                store_state(m_a_ref, 0, r, n, m_a)
                store_state(m_b_ref, 3, r, n, m_b)
            if n > 0:
                m_a = jnp.concatenate([m_a, m_a], axis=1)
                m_b = jnp.concatenate([m_b, m_b], axis=1)
            return jnp.exp2(s_a - m_a).astype(BF16), jnp.exp2(s_b - m_b).astype(BF16), al_a, al_b

        def values(r, n, p_a, p_b, al_a, al_b):
            pv_a = _dot(p_a, va_ref[kv_rows(r, n), :])
            pv_b = _dot(p_b, vb_ref[kv_rows(r, n), :])
            if gi > 0:
                pv_a = al_a * load_state(acc_a_ref, r, n) + pv_a
                pv_b = al_b * load_state(acc_b_ref, r, n) + pv_b
            if gi < n_groups - 1:
                store_state(acc_a_ref, 1, r, n, pv_a)
                store_state(acc_b_ref, 2, r, n, pv_b)
            else:
                tmp_ref[by_s_rows(dil, r, n), :] = jnp.where(
                    in_a_v, pv_a / pltpu.roll(pv_a, E, axis=1), pv_b / pltpu.roll(pv_b, E, axis=1))

        s_val, p_val = {}, {}
        for step in range(len(blocks) + 2):
            if step < len(blocks):
                s_val[step] = scores(*blocks[step])
            if 0 <= step - 1 < len(blocks):
                p_val[step - 1] = softmax(*blocks[step - 1], *s_val.pop(step - 1))
            if 0 <= step - 2 < len(blocks):
                values(*blocks[step - 2], *p_val.pop(step - 2))

        if gi == n_groups - 1:
            for r4 in range(S):
                o_ref[pl.ds(r4, Ls, stride=S), :] = tmp_ref[r4 * Ls:(r4 + 1) * Ls, :]

    for gi, dil in enumerate(ATT_DILATIONS):
        pl.when(grp == gi)(functools.partial(run_group, gi, dil))


def _attn(qkv, B, L, gw):
    N = qkv.shape[0]
    lw = 2 * ATT_HEAD_DIM
    npair = gw // lw
    ng = len(ATT_DILATIONS)

    def spec(kind):
        return pl.BlockSpec((L, lw), lambda b, p, g: (b, kind * ng * npair + g * npair + p))

    return pl.pallas_call(
        _attn_kernel,
        out_shape=jax.ShapeDtypeStruct((N, gw), F32),
        grid=(B, npair, ng),
        in_specs=[spec(0), spec(1), spec(2)],
        out_specs=pl.BlockSpec((L, lw), lambda b, p, g: (b, p)),
        scratch_shapes=[pltpu.VMEM((L, lw), F32)] * 4
                       + [pltpu.VMEM((L, lw), F32),
                          pltpu.VMEM((4, ATT_BLOCK, lw), F32)]
                       + [pltpu.VMEM((L, lw), BF16)] * 5,
        compiler_params=_cparams(3),
        name="dilated_attention",
    )(qkv, qkv, qkv)


def _s5_params(A_re, A_im, log_dt, B_re, B_im, C_re, C_im, Dd):
    G, P = A_re.shape
    dt = jnp.exp(log_dt)[:, None]
    mag = jnp.exp(A_re * dt)
    ab_re = mag * jnp.cos(A_im * dt)
    ab_im = mag * jnp.sin(A_im * dt)
    den = A_re * A_re + A_im * A_im
    nr, ni = ab_re - 1.0, ab_im
    c_re = (nr * A_re + ni * A_im) / den
    c_im = (ni * A_re - nr * A_im) / den
    Bb_re = c_re[..., None] * B_re - c_im[..., None] * B_im
    Bb_im = c_re[..., None] * B_im + c_im[..., None] * B_re
    hg = G // 2
    eye = jnp.eye(hg, dtype=F32)

    def bd_in(m):
        return jnp.einsum('gpc,gh->gchp', m, eye).reshape(hg * S5_GROUP, hg * P)

    def bd_out(m):
        return jnp.einsum('gcp,gh->gphc', m, eye).reshape(hg * P, hg * S5_GROUP)

    bds, cds = [], []
    for k in range(2):
        sl = slice(k * hg, (k + 1) * hg)
        bds.append(jnp.concatenate([bd_in(Bb_re[sl]), bd_in(Bb_im[sl])], axis=1).astype(BF16))
        cds.append(jnp.concatenate([bd_out(C_re[sl]), -bd_out(C_im[sl])], axis=0).astype(BF16))
    return bds[0], bds[1], ab_re.reshape(1, G * P), ab_im.reshape(1, G * P), cds[0], cds[1], Dd.reshape(1, -1)


ROPE_COLS = 32


def _rope_tables(positions):
    half = ROT_DIM // 2
    inv_freq = ROPE_THETA ** (-jnp.arange(half, dtype=F32) * 2.0 / ROT_DIM)
    ang = positions.astype(F32).reshape(-1, 1) * inv_freq
    n = ang.shape[0]
    cs = jnp.concatenate([jnp.cos(ang), jnp.sin(ang), jnp.ones((n, 1), F32),
                          jnp.zeros((n, ROPE_COLS - 2 * half - 1), F32)], axis=1)
    ex = np.zeros((ROPE_COLS, 2 * LANES), np.float32)
    for lane, (_, d) in enumerate(_rot_lane_layout()):
        if d < half:
            ex[d, lane] = 1.0
            ex[half + d, LANES + lane] = -1.0
        elif d < ROT_DIM:
            ex[d - half, lane] = 1.0
            ex[half + d - half, LANES + lane] = 1.0
        else:
            ex[2 * half, lane] = 1.0
    return cs, jnp.asarray(ex, dtype=BF16)


def _qk_lane_layout(w):
    rows, n_cols = w.shape
    E = ATT_HEAD_DIM
    wp = w.reshape(rows, n_cols // LANES, 2, E)
    lay = _rot_lane_layout()
    pieces, start = [], 0
    for l in range(1, LANES + 1):
        if l == LANES or lay[l][0] != lay[start][0] or lay[l][1] != lay[l - 1][1] + 1:
            h, d0 = lay[start]
            pieces.append(wp[:, :, h, d0:d0 + (l - start)])
            start = l
    return jnp.concatenate(pieces, axis=-1).reshape(rows, n_cols)


def kernel(x, positions, norm_mix, norm_ffn, norm_final, mix_w_in, mix_w_out, s5_A_re, s5_A_im, s5_log_dt, s5_B_re, s5_B_im, s5_C_re, s5_C_im, s5_D, s5_glu_w, s5_glu_b, hgrn_gamma, hgrn_norm, att_w_qkv, att_w_o, ffn_w_in, ffn_conv_w, ffn_conv_b, ffn_w_out):
    B, L, D = x.shape
    N = B * L
    x2 = x.reshape(N, D)
    tm = 512
    s5w = s5_A_re.shape[1] * S5_GROUP
    hgw = hgrn_norm.shape[1]

    proj = _norm_proj(x2, norm_mix[0:1], mix_w_in[0].astype(BF16), tm, s5w, name="inproj")
    s5p = _s5_params(s5_A_re[0], s5_A_im[0], s5_log_dt[0], s5_B_re[0], s5_B_im[0],
                     s5_C_re[0], s5_C_im[0], s5_D[0])
    oa = _s5(proj.reshape(B, L, -1), *s5p, s5_glu_w[0].astype(BF16), s5_glu_b[0:1], s5w)
    lb_all = jnp.cumsum(jax.nn.softmax(hgrn_gamma.astype(F32), axis=0), axis=0)
    ob = _hgrn(proj, lb_all[0:1], hgrn_norm[0:1], B, L, col0=s5w // hgw)
    ffn_args = (norm_ffn[:, None, :], ffn_w_in.astype(BF16), ffn_conv_w, ffn_conv_b[:, None, :],
                ffn_w_out.astype(BF16), norm_final.reshape(1, D), L)
    h = _mix_ffn(x2, [oa.reshape(N, s5w), ob], mix_w_out[0].astype(BF16), 0, *ffn_args, final_norm=False)

    gw = att_w_o.shape[1]
    n_rot_cols = 2 * len(ATT_DILATIONS) * gw
    w_qkv = att_w_qkv[0].astype(BF16)
    w_qkv = jnp.concatenate([_qk_lane_layout(w_qkv[:, :n_rot_cols]), w_qkv[:, n_rot_cols:]], axis=1)
    qkv = _norm_proj(h, norm_mix[1:2], w_qkv, tm, gw, rot=_rope_tables(positions),
                     n_rot_cols=n_rot_cols, name="qkv_rotary")
    ao = _attn(qkv, B, L, gw)
    h = _mix_ffn(h, [ao], att_w_o[0].astype(BF16), 1, *ffn_args, final_norm=True)
    return h.reshape(B, L, D)
```

```python
import functools
import math

import numpy as np
import jax
import jax.numpy as jnp
from jax import lax
from jax.experimental import pallas as pl
from jax.experimental.pallas import tpu as pltpu

F32 = jnp.float32
BF16 = jnp.bfloat16

NORM_EPS = 1e-6
S5_GROUP = 16
S5_STATE = 64
HG_HEADS = 4
HG_BLOCK = 128
HG_LAG_BLOCK = 4
ATT_HEAD_DIM = 64
ATT_BLOCK = 128
ATT_DILATIONS = (1, 4, 16)
ROT_DIM = 16
ROPE_THETA = 500000.0
CONV_W = 3
NEG_BIG = -1e30
LOG2E = math.log2(math.e)
LANES = 128

VMEM_LIMIT = 60 * 1024 * 1024


def _cparams(n_axes):
    return pltpu.CompilerParams(dimension_semantics=("arbitrary",) * n_axes,
                                vmem_limit_bytes=VMEM_LIMIT)


def _resident(shape):
    return pl.BlockSpec(shape, lambda *_: (0,) * len(shape), pipeline_mode=pl.Buffered(1))


def _resident_layer(shape, layer):
    return pl.BlockSpec((None,) + tuple(shape[1:]), lambda *_: (layer,) + (0,) * (len(shape) - 1),
                        pipeline_mode=pl.Buffered(1))


def _rms(x, g):
    return x * lax.rsqrt(jnp.mean(x * x, axis=-1, keepdims=True) + NORM_EPS) * g


def _sigmoid(x):
    return 1.0 / (1.0 + jnp.exp(-x))


def _dot(a, b):
    return jnp.dot(a, b, preferred_element_type=F32)


def _dot_nt(a, b):
    return lax.dot_general(a, b, (((1,), (1,)), ((), ())), preferred_element_type=F32)


def _dot_tn(a, b):
    return lax.dot_general(a, b, (((0,), (0,)), ((), ())), preferred_element_type=F32)


def _norm_proj_kernel(x_ref, g_ref, w_ref, *rest, tn, n_rot_tiles):
    if n_rot_tiles:
        cs_ref, ex_ref, o_ref, xn_ref = rest
        cs = cs_ref[...]
        cs_hi = cs.astype(BF16)
        cs_lo = (cs - cs_hi.astype(F32)).astype(BF16)
        tab = _dot(cs_hi, ex_ref[...]) + _dot(cs_lo, ex_ref[...])
        c = tab[:, :LANES]
        s = tab[:, LANES:]
    else:
        o_ref, xn_ref = rest
    tm = x_ref.shape[0]
    parts = [slice(0, tm // 2), slice(tm // 2, tm)]
    for rows in parts:
        xn_ref[rows, :] = _rms(x_ref[rows, :], g_ref[...]).astype(BF16)
    for j in range(w_ref.shape[1] // tn):
        for rows in parts:
            y = _dot(xn_ref[rows, :], w_ref[:, j * tn:(j + 1) * tn])
            if j < n_rot_tiles:
                for k in range(tn // LANES):
                    t = y[:, k * LANES:(k + 1) * LANES]
                    o_ref[rows, j * tn + k * LANES:j * tn + (k + 1) * LANES] = (
                        t * c[rows, :] + pltpu.roll(t, LANES // 2, axis=1) * s[rows, :])
            else:
                o_ref[rows, j * tn:(j + 1) * tn] = y


def _norm_proj(x2, g, w_bf, tm, tn, rot=None, n_rot_cols=0, name="norm_proj"):
    N, D = x2.shape
    ncols = w_bf.shape[1]
    in_specs = [pl.BlockSpec((tm, D), lambda i: (i, 0)), _resident((1, D)), _resident(w_bf.shape)]
    args = [x2, g, w_bf]
    if rot is not None:
        cs, expand = rot
        in_specs += [pl.BlockSpec((tm, cs.shape[1]), lambda i: (i, 0)), _resident(expand.shape)]
        args += [cs, expand]
    return pl.pallas_call(
        functools.partial(_norm_proj_kernel, tn=tn, n_rot_tiles=n_rot_cols // tn),
        out_shape=jax.ShapeDtypeStruct((N, ncols), F32),
        grid=(N // tm,),
        in_specs=in_specs,
        out_specs=pl.BlockSpec((tm, ncols), lambda i: (i, 0)),
        scratch_shapes=[pltpu.VMEM((tm, D), BF16)],
        compiler_params=_cparams(1),
        name=name,
    )(*args)


def _s5_kernel(u_ref, bd0_ref, bd1_ref, ar_ref, ai_ref, c0_ref, c1_ref, d_ref, gw_ref, gb_ref,
               o_ref, x0_ref, x1_ref, st_ref, pf_ref, pb_ref, *, lane_chunk):
    nb, ts, W = u_ref.shape
    tm = nb * ts
    half = x0_ref.shape[1] // 2

    @pl.when(pl.program_id(0) == 0)
    def _():
        st_ref[...] = jnp.zeros_like(st_ref)
        r = lax.broadcasted_iota(jnp.int32, (tm, tm), 0)
        c = lax.broadcasted_iota(jnp.int32, (tm, tm), 1)
        pf_ref[...] = jnp.where(c == (r % nb) * ts + r // nb, 1.0, 0.0).astype(BF16)
        pb_ref[...] = jnp.where(r == (c % nb) * ts + c // nb, 1.0, 0.0).astype(BF16)

    u_bm = u_ref[...].reshape(tm, W)
    u_hi = u_bm.astype(BF16)
    u_lo = (u_bm - u_hi.astype(F32)).astype(BF16)
    ut_hi = _dot(pf_ref[...], u_hi)
    u = ut_hi + _dot(pf_ref[...], u_lo)
    ub = ut_hi.astype(BF16)
    ku = W // 2
    x0_ref[...] = _dot(ub[:, :ku], bd0_ref[...])
    x1_ref[...] = _dot(ub[:, ku:], bd1_ref[...])

    def scan(k, x_ref):
        for c in range(half // lane_chunk):
            lo = c * lane_chunk
            re = slice(lo, lo + lane_chunk)
            im = slice(half + lo, half + lo + lane_chunk)
            a_re = jnp.broadcast_to(ar_ref[:, k * half + lo:k * half + lo + lane_chunk], (nb, lane_chunk))
            a_im = jnp.broadcast_to(ai_ref[:, k * half + lo:k * half + lo + lane_chunk], (nb, lane_chunk))
            xr, xi = st_ref[2 * k, :, re], st_ref[2 * k + 1, :, re]
            for s in range(ts):
                rows = slice(s * nb, (s + 1) * nb)
                xr, xi = (a_re * xr - a_im * xi + x_ref[rows, re],
                          a_re * xi + a_im * xr + x_ref[rows, im])
                x_ref[rows, re] = xr
                x_ref[rows, im] = xi
            st_ref[2 * k, :, re] = xr
            st_ref[2 * k + 1, :, re] = xi

    scan(0, x0_ref)
    y0 = _dot(x0_ref[...].astype(BF16), c0_ref[...])
    scan(1, x1_ref)
    y1 = _dot(x1_ref[...].astype(BF16), c1_ref[...])
    y = jnp.concatenate([y0, y1], axis=1) + d_ref[...] * u
    z = 0.5 * y * (1.0 + jnp.tanh(math.sqrt(2.0 / math.pi) * (y + 0.044715 * (y * y * y))))
    gate = _sigmoid(_dot(z.astype(BF16), gw_ref[...]) + gb_ref[...])
    o_tm = (z * gate).astype(BF16)
    o_ref[...] = _dot(pb_ref[...], o_tm).astype(BF16).reshape(nb, ts, W)


def _s5(proj3, bd0, bd1, ar, ai, c0, c1, dd, gw, gb, W, ts=64):
    B, L, _ = proj3.shape
    S2 = bd0.shape[1]
    tm = B * ts
    kern = functools.partial(_s5_kernel, lane_chunk=1024)
    return pl.pallas_call(
        kern,
        out_shape=jax.ShapeDtypeStruct((B, L, W), BF16),
        grid=(L // ts,),
        in_specs=[pl.BlockSpec((B, ts, W), lambda t: (0, t, 0))]
                 + [_resident(a.shape) for a in (bd0, bd1, ar, ai, c0, c1, dd, gw, gb)],
        out_specs=pl.BlockSpec((B, ts, W), lambda t: (0, t, 0)),
        scratch_shapes=[pltpu.VMEM((tm, S2), F32), pltpu.VMEM((tm, S2), F32),
                        pltpu.VMEM((4, B, S2 // 2), F32),
                        pltpu.VMEM((tm, tm), BF16), pltpu.VMEM((tm, tm), BF16)],
        compiler_params=_cparams(1),
        name="s5_mixer",
    )(proj3, bd0, bd1, ar, ai, c0, c1, dd, gw, gb)


def _hgrn_kernel(xq_ref, xf_ref, xi_ref, xg_ref, lb_ref, ng_ref, o_ref,
                 st_ref, qe_ref, ke_ref, eb_ref, q_ref, k_ref, f_ref, b_ref, oi_ref):
    tl, W = xq_ref.shape
    C = HG_BLOCK
    hd = W // HG_HEADS
    nblk = tl // C

    Cl = HG_LAG_BLOCK
    level_sizes = []
    while (2 * Cl) << len(level_sizes) <= C:
        level_sizes.append((2 * Cl) << len(level_sizes))

    @pl.when(pl.program_id(1) == 0)
    def _():
        st_ref[...] = jnp.zeros_like(st_ref)

    row = lax.broadcasted_iota(jnp.int32, (tl, tl), 0)
    col = lax.broadcasted_iota(jnp.int32, (tl, tl), 1)
    same = (row // C) == (col // C)
    t_cum = jnp.where(same & (col <= row), 1.0, 0.0).astype(BF16)
    t_blk = jnp.where(same, 1.0, 0.0).astype(BF16)

    xq = xq_ref[...]
    q = xq * _sigmoid(xq)
    lb = lb_ref[...]
    f = lb + (1.0 - lb) * _sigmoid(xf_ref[...])
    kk = 1.0 - f
    g2 = jnp.log(f) * LOG2E

    g_hi = g2.astype(BF16)
    g_lo = (g2 - g_hi.astype(F32)).astype(BF16)
    b = _dot(t_cum, g_hi) + _dot(t_cum, g_lo)
    bl = _dot(t_blk, g_hi) + _dot(t_blk, g_lo)

    q_ref[...] = q
    k_ref[...] = kk
    f_ref[...] = f
    b_ref[...] = b
    qe_ref[...] = (q * jnp.exp2(b)).astype(BF16)
    ke_ref[...] = (kk * jnp.exp2(bl - b)).astype(BF16)
    eb_ref[...] = jnp.exp2(bl)

    heads = [slice(h * hd, (h + 1) * hd) for h in range(HG_HEADS)]
    st = [st_ref[h] for h in range(HG_HEADS)]
    for i in range(nblk):
        rows = slice(i * C, (i + 1) * C)
        for h, hl in enumerate(heads):
            oi_ref[rows, hl] = _dot_nt(qe_ref[rows, hl], st[h].astype(BF16))
            vb = xi_ref[rows, hl].astype(BF16)
            st[h] = st[h] * eb_ref[i * C:i * C + 1, hl] + _dot_tn(vb, ke_ref[rows, hl])
    for h in range(HG_HEADS):
        st_ref[h] = st[h]

    levels = []
    for size in level_sizes:
        half = size // 2
        anchor = jnp.concatenate(
            [jnp.broadcast_to(b_ref[j * size + half - 1:j * size + half, :], (size, W))
             for j in range(tl // size)], axis=0)
        eh = jnp.exp2(-jnp.abs(b_ref[...] - anchor))
        pair_ok = ((row // size) == (col // size)) & ((row % size) >= half) & ((col % size) < half)
        levels.append(((q_ref[...] * eh).astype(BF16), (k_ref[...] * eh).astype(BF16), pair_ok))
    lvl = []
    for hl in heads:
        w = jnp.zeros((tl, tl), F32)
        for qh, kh, pair_ok in levels:
            w = jnp.where(pair_ok, _dot_nt(qh[:, hl], kh[:, hl]), w)
        lvl.append(w)

    rc = 64 if tl % 64 == 0 else tl
    r_i = lax.broadcasted_iota(jnp.int32, (rc, tl), 0)
    c_i = lax.broadcasted_iota(jnp.int32, (rc, tl), 1)
    for r0 in range(0, tl, rc):
        lag = jnp.where(((r_i + r0) // Cl) == (c_i // Cl), r_i + r0 - c_i, -1)
        rows = slice(r0, r0 + rc)
        p = [k_ref[rows, hl] for hl in heads]
        a_mat = [jnp.zeros((rc, tl), F32) for _ in heads]
        for d in range(Cl):
            for h, hl in enumerate(heads):
                if d > 0:
                    p[h] = f_ref[rows, hl] * pltpu.roll(p[h], 1, axis=0)
                a = jnp.sum(q_ref[rows, hl] * p[h], axis=-1, keepdims=True)
                a_mat[h] = jnp.where(lag == d, a, a_mat[h])
        for h, hl in enumerate(heads):
            w = (a_mat[h] + lvl[h][r0:r0 + rc, :]).astype(BF16)
            acc = oi_ref[rows, hl] + _dot(w, xi_ref[:, hl].astype(BF16))
            on = acc * lax.rsqrt(jnp.mean(acc * acc, axis=-1, keepdims=True) + NORM_EPS)
            xg = xg_ref[rows, hl]
            o_ref[rows, hl] = (on * ng_ref[:, hl] * (xg * _sigmoid(xg))).astype(o_ref.dtype)


def _hgrn(proj, lb, ng, B, L, col0, tl=128):
    N = proj.shape[0]
    W = lb.shape[1]
    hd = W // HG_HEADS
    tps = L // tl
    col = lambda c: pl.BlockSpec((tl, W), lambda b, t: (b * tps + t, col0 + c))
    return pl.pallas_call(
        _hgrn_kernel,
        out_shape=jax.ShapeDtypeStruct((N, W), BF16),
        grid=(B, tps),
        in_specs=[col(0), col(1), col(2), col(3), _resident((1, W)), _resident((1, W))],
        out_specs=pl.BlockSpec((tl, W), lambda b, t: (b * tps + t, 0)),
        scratch_shapes=[pltpu.VMEM((HG_HEADS, hd, hd), F32),
                        pltpu.VMEM((tl, W), BF16),
                        pltpu.VMEM((tl, W), BF16),
                        pltpu.VMEM((tl, W), F32),
                        pltpu.VMEM((tl, W), F32),
                        pltpu.VMEM((tl, W), F32),
                        pltpu.VMEM((tl, W), F32),
                        pltpu.VMEM((tl, W), F32),
                        pltpu.VMEM((tl, W), F32)],
        compiler_params=_cparams(2),
        name="hgrn2_mixer",
    )(proj, proj, proj, proj, lb, ng)


FFN_HALO = 16
FFN_PART_ROWS = 256


def _ffn_kernel(*refs, n_mix, tiles_per_seq, final_norm, tf):
    x_ref, xhalo_ref = refs[:2]
    mix_refs = refs[2:2 + n_mix]
    mixhalo_refs = refs[2 + n_mix:2 + 2 * n_mix]
    (wp_ref, g_ref, win_ref, cw_ref, cb_ref, wo_ref, gf_ref, o_ref,
     xn_ref, act_ref, a_ref, b_ref, h_ref) = refs[2 + 2 * n_mix:]
    i = pl.program_id(0)
    tm = x_ref.shape[0]
    H = FFN_HALO
    dff = wo_ref.shape[0]

    def mixed(r_ref, m_refs, rows):
        acc = r_ref[rows, :]
        k0 = 0
        for m_ref in m_refs:
            kw = m_ref.shape[1]
            acc = acc + _dot(m_ref[rows, :].astype(BF16), wp_ref[k0:k0 + kw, :])
            k0 += kw
        return acc

    hm = FFN_PART_ROWS
    parts = [(lo, lo + hm) for lo in range(0, tm, hm)]
    hn = _rms(mixed(xhalo_ref, mixhalo_refs, slice(0, H)), g_ref[...])
    xn_ref[0:H, :] = jnp.where(i % tiles_per_seq == 0, 0.0, hn).astype(BF16)
    for lo, hi in parts:
        h_ref[lo:hi, :] = mixed(x_ref, mix_refs, slice(lo, hi))
        xn_ref[H + lo:H + hi, :] = _rms(h_ref[lo:hi, :], g_ref[...]).astype(BF16)

    def conv(s_ref, c0, lo, hi):
        cols = slice(c0, c0 + tf)
        out = cb_ref[:, cols] + s_ref[H + lo:H + hi, :] * cw_ref[CONV_W - 1:CONV_W, cols]
        for t in range(1, CONV_W):
            out = out + s_ref[H + lo - t:H + hi - t, :] * cw_ref[CONV_W - 1 - t:CONV_W - t, cols]
        return out

    def up(j):
        for lo, hi in parts:
            strip = slice(0 if lo == 0 else H + lo, H + hi)
            a_ref[j % 2, strip, :] = _dot(xn_ref[strip, :], win_ref[:, j * tf:(j + 1) * tf])
            b_ref[j % 2, strip, :] = _dot(xn_ref[strip, :], win_ref[:, dff + j * tf:dff + (j + 1) * tf])

    def activate(j):
        for lo, hi in parts:
            a = conv(a_ref.at[j % 2], j * tf, lo, hi)
            b = conv(b_ref.at[j % 2], dff + j * tf, lo, hi)
            act_ref[lo:hi, j * tf:(j + 1) * tf] = (a * _sigmoid(a) * b).astype(BF16)

    nj = dff // tf
    for step in range(nj + 1):
        if step < nj:
            up(step)
        if step >= 1:
            activate(step - 1)

    for lo, hi in parts:
        y = h_ref[lo:hi, :] + _dot(act_ref[lo:hi, :], wo_ref[...])
        if final_norm:
            y = _rms(y, gf_ref[...])
        o_ref[lo:hi, :] = y


def _mix_ffn(x, mix_list, w_proj_bf, layer, g, w_in_bf, conv_w, conv_b, w_out_bf, g_final, L, final_norm,
             tm=1024, tf=256):
    N, D = x.shape
    dff = w_out_bf.shape[1]
    H = FFN_HALO
    main = lambda w: pl.BlockSpec((tm, w), lambda i: (i, 0))
    halo = lambda w: pl.BlockSpec((H, w), lambda i: (jnp.maximum(i * (tm // H) - 1, 0), 0))
    kern = functools.partial(_ffn_kernel, n_mix=len(mix_list), tiles_per_seq=L // tm,
                             final_norm=final_norm, tf=tf)
    return pl.pallas_call(
        kern,
        out_shape=jax.ShapeDtypeStruct((N, D), F32),
        grid=(N // tm,),
        in_specs=[main(D), halo(D)]
                 + [main(m.shape[1]) for m in mix_list] + [halo(m.shape[1]) for m in mix_list]
                 + [_resident(w_proj_bf.shape)]
                 + [_resident_layer(a.shape, layer) for a in (g, w_in_bf, conv_w, conv_b, w_out_bf)]
                 + [_resident((1, D))],
        out_specs=pl.BlockSpec((tm, D), lambda i: (i, 0)),
        scratch_shapes=[pltpu.VMEM((tm + H, D), BF16),
                        pltpu.VMEM((tm, dff), BF16),
                        pltpu.VMEM((2, tm + H, tf), F32),
                        pltpu.VMEM((2, tm + H, tf), F32),
                        pltpu.VMEM((tm, D), F32)],
        compiler_params=_cparams(1),
        name="mix_conv_ffn",
    )(x, x, *mix_list, *mix_list, w_proj_bf, g, w_in_bf, conv_w, conv_b, w_out_bf, g_final)


def _rot_lane_layout():
    half = ROT_DIM // 2
    E = ATT_HEAD_DIM
    lay = []
    for base in (0, half):
        lay += [(0, base + i) for i in range(half)] + [(1, base + i) for i in range(half)]
        lay += [(base // half, d) for d in range(ROT_DIM, E)]
    return lay


def _attn_kernel(q_ref, k_ref, v_ref, o_ref, m_a_ref, m_b_ref, acc_a_ref, acc_b_ref, tmp_ref,
                 qs_ref, ka_ref, kb_ref, va_ref, vb_ref):
    grp = pl.program_id(2)
    L, lw = q_ref.shape
    T = ATT_BLOCK
    E = ATT_HEAD_DIM
    n_groups = len(ATT_DILATIONS)

    lane1 = lax.broadcasted_iota(jnp.int32, (1, lw), 1)
    head_of_lane = np.array([h for h, _ in _rot_lane_layout()])
    in_a_qk = jnp.zeros((1, lw), jnp.bool_)
    for lo, hi in ((0, 8), (16, 72)):
        in_a_qk = in_a_qk | ((lane1 >= lo) & (lane1 < hi))
    assert all((head_of_lane[l] == 0) == (l < 8 or 16 <= l < 72) for l in range(lw))
    in_a_v = lane1 < E
    qi = lax.broadcasted_iota(jnp.int32, (T, T), 0)
    kj = lax.broadcasted_iota(jnp.int32, (T, T), 1)

    S = ATT_DILATIONS[1]
    assert ATT_DILATIONS == (1, S, S * S) and T % S == 0
    Ls = L // S

    def by_s_rows(dil, r, n):
        if dil == S:
            return slice(r * Ls + n * T, r * Ls + (n + 1) * T)
        return pl.ds((r % S) * Ls + r // S, T, stride=S)

    def run_group(gi, dil):
        M = L // dil
        nblk = M // T

        def residue_major(x_ref, emit):
            if dil == S * S:
                for r4 in range(S):
                    tmp_ref[r4 * Ls:(r4 + 1) * Ls, :] = x_ref[pl.ds(r4, Ls, stride=S), :]
            for r in range(dil):
                if dil == 1:
                    xr = x_ref[...]
                elif dil == S:
                    xr = x_ref[pl.ds(r, M, stride=S), :]
                else:
                    xr = tmp_ref[by_s_rows(dil, r, 0), :]
                emit(slice(r * M, (r + 1) * M), xr)

        def emit_q(dst, xr):
            qs_ref[dst, :] = (xr * (E ** -0.5 * LOG2E)).astype(BF16)

        def emit_k(dst, xr):
            ka_ref[dst, :] = jnp.where(in_a_qk, xr, 0.0).astype(BF16)
            kb_ref[dst, :] = jnp.where(in_a_qk, 0.0, xr).astype(BF16)

        def emit_v(dst, xr):
            va_ref[dst, :] = jnp.where(in_a_v, xr, 1.0).astype(BF16)
            vb_ref[dst, :] = jnp.where(in_a_v, 1.0, xr).astype(BF16)

        if dil == 1:
            for n in range(nblk):
                for j in range(S):
                    emit_q(slice(n * T + j * (T // S), n * T + (j + 1) * (T // S)),
                           q_ref[pl.ds(n * T + j, T // S, stride=S), :])
            q_pos = S * (qi % (T // S)) + qi // (T // S)
        else:
            residue_major(q_ref, emit_q)
            q_pos = qi
        cur_ok = kj <= q_pos
        both_ok = jnp.concatenate([kj >= q_pos, cur_ok], axis=1)
        residue_major(k_ref, emit_k)
        residue_major(v_ref, emit_v)

        k_refs, v_refs = (ka_ref, kb_ref), (va_ref, vb_ref)
        m_refs, acc_refs = (m_a_ref, m_b_ref), (acc_a_ref, acc_b_ref)
        blocks = [(r, n) for r in range(dil) for n in range(nblk)]

        def load_state(ref, r, n):
            return ref[by_s_rows(dil, r, n), :]

        def store_state(ref, r, n, val):
            if dil == 1:
                for j in range(S):
                    ref[j * Ls + (T // S) * n:j * Ls + (T // S) * (n + 1), :] = (
                        val[j * (T // S):(j + 1) * (T // S), :])
            else:
                ref[by_s_rows(dil, r, n), :] = val

        def kv_rows(r, n):
            c0 = r * M + n * T
            return slice(c0 - T, c0 + T) if n > 0 else slice(c0, c0 + T)

        def scores(r, n, hh):
            c0 = r * M + n * T
            return _dot_nt(qs_ref[c0:c0 + T, :], k_refs[hh][kv_rows(r, n), :])

        def softmax(r, n, hh, s):
            s = jnp.where(both_ok if n > 0 else cur_ok, s, NEG_BIG)
            m = jnp.broadcast_to(jnp.max(s, axis=-1, keepdims=True), (T, lw))
            alpha = None
            if gi > 0:
                m_old = load_state(m_refs[hh], r, n)
                m = jnp.maximum(m, m_old)
                alpha = jnp.exp2(m_old - m)
            if gi < n_groups - 1:
                store_state(m_refs[hh], r, n, m)
            if n > 0:
                m = jnp.concatenate([m, m], axis=1)
            return jnp.exp2(s - m).astype(BF16), alpha

        pv_first = {}

        def values(r, n, hh, p, alpha):
            pv = _dot(p, v_refs[hh][kv_rows(r, n), :])
            if gi > 0:
                pv = alpha * load_state(acc_refs[hh], r, n) + pv
            if gi < n_groups - 1:
                store_state(acc_refs[hh], r, n, pv)
            elif hh == 0:
                pv_first[(r, n)] = pv / pltpu.roll(pv, E, axis=1)
            else:
                tmp_ref[by_s_rows(dil, r, n), :] = jnp.where(
                    in_a_v, pv_first.pop((r, n)), pv / pltpu.roll(pv, E, axis=1))

        s_val, p_val = {}, {}
        for step in range(len(blocks) + 3):
            if step < len(blocks):
                s_val[step] = [scores(*blocks[step], hh) for hh in range(2)]
            if 0 <= step - 1 < len(blocks):
                s_pair = s_val.pop(step - 1)
                p_val[step - 1] = [softmax(*blocks[step - 1], hh, s_pair[hh]) for hh in range(2)]
            if 0 <= step - 3 < len(blocks):
                p_pair = p_val.pop(step - 3)
                for hh in range(2):
                    values(*blocks[step - 3], hh, *p_pair[hh])

        if gi == n_groups - 1:
            for r4 in range(S):
                o_ref[pl.ds(r4, Ls, stride=S), :] = tmp_ref[r4 * Ls:(r4 + 1) * Ls, :]

    for gi, dil in enumerate(ATT_DILATIONS):
        pl.when(grp == gi)(functools.partial(run_group, gi, dil))


def _attn(qkv, B, L, gw):
    N = qkv.shape[0]
    lw = 2 * ATT_HEAD_DIM
    npair = gw // lw
    ng = len(ATT_DILATIONS)

    def spec(kind):
        return pl.BlockSpec((L, lw), lambda b, p, g: (b, kind * ng * npair + g * npair + p))

    return pl.pallas_call(
        _attn_kernel,
        out_shape=jax.ShapeDtypeStruct((N, gw), F32),
        grid=(B, npair, ng),
        in_specs=[spec(0), spec(1), spec(2)],
        out_specs=pl.BlockSpec((L, lw), lambda b, p, g: (b, p)),
        scratch_shapes=[pltpu.VMEM((L, lw), F32)] * 4
                       + [pltpu.VMEM((L, lw), F32)]
                       + [pltpu.VMEM((L, lw), BF16)] * 5,
        compiler_params=_cparams(3),
        name="dilated_attention",
    )(qkv, qkv, qkv)


def _s5_params(A_re, A_im, log_dt, B_re, B_im, C_re, C_im, Dd):
    G, P = A_re.shape
    dt = jnp.exp(log_dt)[:, None]
    mag = jnp.exp(A_re * dt)
    ab_re = mag * jnp.cos(A_im * dt)
    ab_im = mag * jnp.sin(A_im * dt)
    den = A_re * A_re + A_im * A_im
    nr, ni = ab_re - 1.0, ab_im
    c_re = (nr * A_re + ni * A_im) / den
    c_im = (ni * A_re - nr * A_im) / den
    Bb_re = c_re[..., None] * B_re - c_im[..., None] * B_im
    Bb_im = c_re[..., None] * B_im + c_im[..., None] * B_re
    hg = G // 2
    eye = jnp.eye(hg, dtype=F32)

    def bd_in(m):
        return jnp.einsum('gpc,gh->gchp', m, eye).reshape(hg * S5_GROUP, hg * P)

    def bd_out(m):
        return jnp.einsum('gcp,gh->gphc', m, eye).reshape(hg * P, hg * S5_GROUP)

    bds, cds = [], []
    for k in range(2):
        sl = slice(k * hg, (k + 1) * hg)
        bds.append(jnp.concatenate([bd_in(Bb_re[sl]), bd_in(Bb_im[sl])], axis=1).astype(BF16))
        cds.append(jnp.concatenate([bd_out(C_re[sl]), -bd_out(C_im[sl])], axis=0).astype(BF16))
    return bds[0], bds[1], ab_re.reshape(1, G * P), ab_im.reshape(1, G * P), cds[0], cds[1], Dd.reshape(1, -1)


ROPE_COLS = 32


def _rope_tables(positions):
    half = ROT_DIM // 2
    inv_freq = ROPE_THETA ** (-jnp.arange(half, dtype=F32) * 2.0 / ROT_DIM)
    ang = positions.astype(F32).reshape(-1, 1) * inv_freq
    n = ang.shape[0]
    cs = jnp.concatenate([jnp.cos(ang), jnp.sin(ang), jnp.ones((n, 1), F32),
                          jnp.zeros((n, ROPE_COLS - 2 * half - 1), F32)], axis=1)
    ex = np.zeros((ROPE_COLS, 2 * LANES), np.float32)
    for lane, (_, d) in enumerate(_rot_lane_layout()):
        if d < half:
            ex[d, lane] = 1.0
            ex[half + d, LANES + lane] = -1.0
        elif d < ROT_DIM:
            ex[d - half, lane] = 1.0
            ex[half + d - half, LANES + lane] = 1.0
        else:
            ex[2 * half, lane] = 1.0
    return cs, jnp.asarray(ex, dtype=BF16)


def _qk_lane_layout(w):
    rows, n_cols = w.shape
    E = ATT_HEAD_DIM
    wp = w.reshape(rows, n_cols // LANES, 2, E)
    lay = _rot_lane_layout()
    pieces, start = [], 0
    for l in range(1, LANES + 1):
        if l == LANES or lay[l][0] != lay[start][0] or lay[l][1] != lay[l - 1][1] + 1:
            h, d0 = lay[start]
            pieces.append(wp[:, :, h, d0:d0 + (l - start)])
            start = l
    return jnp.concatenate(pieces, axis=-1).reshape(rows, n_cols)


def kernel(x, positions, norm_mix, norm_ffn, norm_final, mix_w_in, mix_w_out, s5_A_re, s5_A_im, s5_log_dt, s5_B_re, s5_B_im, s5_C_re, s5_C_im, s5_D, s5_glu_w, s5_glu_b, hgrn_gamma, hgrn_norm, att_w_qkv, att_w_o, ffn_w_in, ffn_conv_w, ffn_conv_b, ffn_w_out):
    B, L, D = x.shape
    N = B * L
    x2 = x.reshape(N, D)
    tm = 512
    s5w = s5_A_re.shape[1] * S5_GROUP
    hgw = hgrn_norm.shape[1]

    proj = _norm_proj(x2, norm_mix[0:1], mix_w_in[0].astype(BF16), tm, s5w, name="inproj")
    s5p = _s5_params(s5_A_re[0], s5_A_im[0], s5_log_dt[0], s5_B_re[0], s5_B_im[0],
                     s5_C_re[0], s5_C_im[0], s5_D[0])
    oa = _s5(proj.reshape(B, L, -1), *s5p, s5_glu_w[0].astype(BF16), s5_glu_b[0:1], s5w)
    lb_all = jnp.cumsum(jax.nn.softmax(hgrn_gamma.astype(F32), axis=0), axis=0)
    ob = _hgrn(proj, lb_all[0:1], hgrn_norm[0:1], B, L, col0=s5w // hgw)
    ffn_args = (norm_ffn[:, None, :], ffn_w_in.astype(BF16), ffn_conv_w, ffn_conv_b[:, None, :],
                ffn_w_out.astype(BF16), norm_final.reshape(1, D), L)
    h = _mix_ffn(x2, [oa.reshape(N, s5w), ob], mix_w_out[0].astype(BF16), 0, *ffn_args, final_norm=False)

    gw = att_w_o.shape[1]
    n_rot_cols = 2 * len(ATT_DILATIONS) * gw
    w_qkv = att_w_qkv[0].astype(BF16)
    w_qkv = jnp.concatenate([_qk_lane_layout(w_qkv[:, :n_rot_cols]), w_qkv[:, n_rot_cols:]], axis=1)
    qkv = _norm_proj(h, norm_mix[1:2], w_qkv, tm, gw, rot=_rope_tables(positions),
                     n_rot_cols=n_rot_cols, name="qkv_rotary")
    ao = _attn(qkv, B, L, gw)
    h = _mix_ffn(h, [ao], att_w_o[0].astype(BF16), 1, *ffn_args, final_norm=True)
    return h.reshape(B, L, D)
```

```python
import functools
import math

import numpy as np
import jax
import jax.numpy as jnp
from jax import lax
from jax.experimental import pallas as pl
from jax.experimental.pallas import tpu as pltpu

F32 = jnp.float32
BF16 = jnp.bfloat16

NORM_EPS = 1e-6
S5_GROUP = 16
S5_STATE = 64
HG_HEADS = 4
HG_BLOCK = 128
HG_LAG_BLOCK = 4
ATT_HEAD_DIM = 64
ATT_BLOCK = 128
ATT_DILATIONS = (1, 4, 16)
ROT_DIM = 16
ROPE_THETA = 500000.0
CONV_W = 3
NEG_BIG = -1e30
LOG2E = math.log2(math.e)
LANES = 128

PROJ_ROW_TILE = 512
FFN_ROW_TILE = 1024
FFN_COL_TILE = 256
FFN_PART_ROWS = 256
FFN_HALO = 16
S5_TIME_TILE = 64
S5_LANE_CHUNK = 1024
HG_ROW_TILE = 128
HG_ROW_CHUNK = 64

VMEM_PHYSICAL = 64 * 1024 * 1024
VMEM_LIMIT = VMEM_PHYSICAL - 4 * 1024 * 1024


def _cparams(n_axes):
    return pltpu.CompilerParams(dimension_semantics=("arbitrary",) * n_axes,
                                vmem_limit_bytes=VMEM_LIMIT)


def _resident(shape):
    return pl.BlockSpec(shape, lambda *_: (0,) * len(shape), pipeline_mode=pl.Buffered(1))


def _resident_layer(shape, layer):
    return pl.BlockSpec((None,) + tuple(shape[1:]), lambda *_: (layer,) + (0,) * (len(shape) - 1),
                        pipeline_mode=pl.Buffered(1))


def _rms(x, g):
    return x * lax.rsqrt(jnp.mean(x * x, axis=-1, keepdims=True) + NORM_EPS) * g


def _sigmoid(x):
    return 1.0 / (1.0 + jnp.exp(-x))


def _dot(a, b):
    return jnp.dot(a, b, preferred_element_type=F32)


def _dot_nt(a, b):
    return lax.dot_general(a, b, (((1,), (1,)), ((), ())), preferred_element_type=F32)


def _dot_tn(a, b):
    return lax.dot_general(a, b, (((0,), (0,)), ((), ())), preferred_element_type=F32)


def _norm_proj_kernel(x_ref, g_ref, w_ref, *rest, tn, n_rot_tiles):
    if n_rot_tiles:
        cs_ref, ex_ref, o_ref, xn_ref = rest
        cs = cs_ref[...]
        cs_hi = cs.astype(BF16)
        cs_lo = (cs - cs_hi.astype(F32)).astype(BF16)
        tab = _dot(cs_hi, ex_ref[...]) + _dot(cs_lo, ex_ref[...])
        c = tab[:, :LANES]
        s = tab[:, LANES:]
    else:
        o_ref, xn_ref = rest
    tm = x_ref.shape[0]
    parts = [slice(0, tm // 2), slice(tm // 2, tm)]
    for rows in parts:
        xn_ref[rows, :] = _rms(x_ref[rows, :], g_ref[...]).astype(BF16)
    for j in range(w_ref.shape[1] // tn):
        for rows in parts:
            y = _dot(xn_ref[rows, :], w_ref[:, j * tn:(j + 1) * tn])
            if j < n_rot_tiles:
                for k in range(tn // LANES):
                    t = y[:, k * LANES:(k + 1) * LANES]
                    o_ref[rows, j * tn + k * LANES:j * tn + (k + 1) * LANES] = (
                        t * c[rows, :] + pltpu.roll(t, LANES // 2, axis=1) * s[rows, :])
            else:
                o_ref[rows, j * tn:(j + 1) * tn] = y


def _norm_proj(x2, g, w_bf, tm, tn, rot=None, n_rot_cols=0, name="norm_proj"):
    N, D = x2.shape
    ncols = w_bf.shape[1]
    in_specs = [pl.BlockSpec((tm, D), lambda i: (i, 0)), _resident((1, D)), _resident(w_bf.shape)]
    args = [x2, g, w_bf]
    if rot is not None:
        cs, expand = rot
        in_specs += [pl.BlockSpec((tm, cs.shape[1]), lambda i: (i, 0)), _resident(expand.shape)]
        args += [cs, expand]
    return pl.pallas_call(
        functools.partial(_norm_proj_kernel, tn=tn, n_rot_tiles=n_rot_cols // tn),
        out_shape=jax.ShapeDtypeStruct((N, ncols), F32),
        grid=(N // tm,),
        in_specs=in_specs,
        out_specs=pl.BlockSpec((tm, ncols), lambda i: (i, 0)),
        scratch_shapes=[pltpu.VMEM((tm, D), BF16)],
        compiler_params=_cparams(1),
        name=name,
    )(*args)


def _s5_kernel(u_ref, bd0_ref, bd1_ref, ar_ref, ai_ref, c0_ref, c1_ref, d_ref, gw_ref, gb_ref,
               o_ref, x0_ref, x1_ref, st_ref, pf_ref, pb_ref, *, lane_chunk):
    nb, ts, W = u_ref.shape
    tm = nb * ts
    half = x0_ref.shape[1] // 2

    @pl.when(pl.program_id(0) == 0)
    def _():
        st_ref[...] = jnp.zeros_like(st_ref)
        r = lax.broadcasted_iota(jnp.int32, (tm, tm), 0)
        c = lax.broadcasted_iota(jnp.int32, (tm, tm), 1)
        pf_ref[...] = jnp.where(c == (r % nb) * ts + r // nb, 1.0, 0.0).astype(BF16)
        pb_ref[...] = jnp.where(r == (c % nb) * ts + c // nb, 1.0, 0.0).astype(BF16)

    u_bm = u_ref[...].reshape(tm, W)
    u_hi = u_bm.astype(BF16)
    u_lo = (u_bm - u_hi.astype(F32)).astype(BF16)
    ut_hi = _dot(pf_ref[...], u_hi)
    u = ut_hi + _dot(pf_ref[...], u_lo)
    ub = ut_hi.astype(BF16)
    ku = W // 2
    x0_ref[...] = _dot(ub[:, :ku], bd0_ref[...])
    x1_ref[...] = _dot(ub[:, ku:], bd1_ref[...])

    def scan(k, x_ref):
        for c in range(half // lane_chunk):
            lo = c * lane_chunk
            re = slice(lo, lo + lane_chunk)
            im = slice(half + lo, half + lo + lane_chunk)
            a_re = jnp.broadcast_to(ar_ref[:, k * half + lo:k * half + lo + lane_chunk], (nb, lane_chunk))
            a_im = jnp.broadcast_to(ai_ref[:, k * half + lo:k * half + lo + lane_chunk], (nb, lane_chunk))
            xr, xi = st_ref[2 * k, :, re], st_ref[2 * k + 1, :, re]
            for s in range(ts):
                rows = slice(s * nb, (s + 1) * nb)
                xr, xi = (a_re * xr - a_im * xi + x_ref[rows, re],
                          a_re * xi + a_im * xr + x_ref[rows, im])
                x_ref[rows, re] = xr
                x_ref[rows, im] = xi
            st_ref[2 * k, :, re] = xr
            st_ref[2 * k + 1, :, re] = xi

    scan(0, x0_ref)
    y0 = _dot(x0_ref[...].astype(BF16), c0_ref[...])
    scan(1, x1_ref)
    y1 = _dot(x1_ref[...].astype(BF16), c1_ref[...])
    y = jnp.concatenate([y0, y1], axis=1) + d_ref[...] * u
    z = 0.5 * y * (1.0 + jnp.tanh(math.sqrt(2.0 / math.pi) * (y + 0.044715 * (y * y * y))))
    gate = _sigmoid(_dot(z.astype(BF16), gw_ref[...]) + gb_ref[...])
    o_tm = (z * gate).astype(BF16)
    o_ref[...] = _dot(pb_ref[...], o_tm).astype(BF16).reshape(nb, ts, W)


def _s5(proj3, bd0, bd1, ar, ai, c0, c1, dd, gw, gb, W, ts=S5_TIME_TILE):
    B, L, _ = proj3.shape
    S2 = bd0.shape[1]
    tm = B * ts
    kern = functools.partial(_s5_kernel, lane_chunk=S5_LANE_CHUNK)
    return pl.pallas_call(
        kern,
        out_shape=jax.ShapeDtypeStruct((B, L, W), BF16),
        grid=(L // ts,),
        in_specs=[pl.BlockSpec((B, ts, W), lambda t: (0, t, 0))]
                 + [_resident(a.shape) for a in (bd0, bd1, ar, ai, c0, c1, dd, gw, gb)],
        out_specs=pl.BlockSpec((B, ts, W), lambda t: (0, t, 0)),
        scratch_shapes=[pltpu.VMEM((tm, S2), F32), pltpu.VMEM((tm, S2), F32),
                        pltpu.VMEM((4, B, S2 // 2), F32),
                        pltpu.VMEM((tm, tm), BF16), pltpu.VMEM((tm, tm), BF16)],
        compiler_params=_cparams(1),
        name="s5_mixer",
    )(proj3, bd0, bd1, ar, ai, c0, c1, dd, gw, gb)


def _hgrn_kernel(xq_ref, xf_ref, xi_ref, xg_ref, lb_ref, ng_ref, o_ref,
                 st_ref, qe_ref, ke_ref, eb_ref, q_ref, k_ref, f_ref, b_ref, oi_ref):
    tl, W = xq_ref.shape
    C = HG_BLOCK
    hd = W // HG_HEADS
    nblk = tl // C

    Cl = HG_LAG_BLOCK
    level_sizes = []
    while (2 * Cl) << len(level_sizes) <= C:
        level_sizes.append((2 * Cl) << len(level_sizes))

    @pl.when(pl.program_id(1) == 0)
    def _():
        st_ref[...] = jnp.zeros_like(st_ref)

    row = lax.broadcasted_iota(jnp.int32, (tl, tl), 0)
    col = lax.broadcasted_iota(jnp.int32, (tl, tl), 1)
    same = (row // C) == (col // C)
    t_cum = jnp.where(same & (col <= row), 1.0, 0.0).astype(BF16)
    t_blk = jnp.where(same, 1.0, 0.0).astype(BF16)

    xq = xq_ref[...]
    q = xq * _sigmoid(xq)
    lb = lb_ref[...]
    f = lb + (1.0 - lb) * _sigmoid(xf_ref[...])
    kk = 1.0 - f
    g2 = jnp.log(f) * LOG2E

    g_hi = g2.astype(BF16)
    g_lo = (g2 - g_hi.astype(F32)).astype(BF16)
    b = _dot(t_cum, g_hi) + _dot(t_cum, g_lo)
    bl = _dot(t_blk, g_hi) + _dot(t_blk, g_lo)

    q_ref[...] = q
    k_ref[...] = kk
    f_ref[...] = f
    b_ref[...] = b
    qe_ref[...] = (q * jnp.exp2(b)).astype(BF16)
    ke_ref[...] = (kk * jnp.exp2(bl - b)).astype(BF16)
    eb_ref[...] = jnp.exp2(bl)

    heads = [slice(h * hd, (h + 1) * hd) for h in range(HG_HEADS)]
    st = [st_ref[h] for h in range(HG_HEADS)]
    for i in range(nblk):
        rows = slice(i * C, (i + 1) * C)
        for h, hl in enumerate(heads):
            oi_ref[rows, hl] = _dot_nt(qe_ref[rows, hl], st[h].astype(BF16))
            vb = xi_ref[rows, hl].astype(BF16)
            st[h] = st[h] * eb_ref[i * C:i * C + 1, hl] + _dot_tn(vb, ke_ref[rows, hl])
    for h in range(HG_HEADS):
        st_ref[h] = st[h]

    levels = []
    for size in level_sizes:
        half = size // 2
        anchor = jnp.concatenate(
            [jnp.broadcast_to(b_ref[j * size + half - 1:j * size + half, :], (size, W))
             for j in range(tl // size)], axis=0)
        eh = jnp.exp2(-jnp.abs(b_ref[...] - anchor))
        pair_ok = ((row // size) == (col // size)) & ((row % size) >= half) & ((col % size) < half)
        levels.append(((q_ref[...] * eh).astype(BF16), (k_ref[...] * eh).astype(BF16), pair_ok))
    lvl = []
    for hl in heads:
        w = jnp.zeros((tl, tl), F32)
        for qh, kh, pair_ok in levels:
            w = jnp.where(pair_ok, _dot_nt(qh[:, hl], kh[:, hl]), w)
        lvl.append(w)

    rc = HG_ROW_CHUNK if tl % HG_ROW_CHUNK == 0 else tl
    r_i = lax.broadcasted_iota(jnp.int32, (rc, tl), 0)
    c_i = lax.broadcasted_iota(jnp.int32, (rc, tl), 1)
    for r0 in range(0, tl, rc):
        lag = jnp.where(((r_i + r0) // Cl) == (c_i // Cl), r_i + r0 - c_i, -1)
        rows = slice(r0, r0 + rc)
        p = [k_ref[rows, hl] for hl in heads]
        a_mat = [jnp.zeros((rc, tl), F32) for _ in heads]
        for d in range(Cl):
            for h, hl in enumerate(heads):
                if d > 0:
                    p[h] = f_ref[rows, hl] * pltpu.roll(p[h], 1, axis=0)
                a = jnp.sum(q_ref[rows, hl] * p[h], axis=-1, keepdims=True)
                a_mat[h] = jnp.where(lag == d, a, a_mat[h])
        for h, hl in enumerate(heads):
            w = (a_mat[h] + lvl[h][r0:r0 + rc, :]).astype(BF16)
            acc = oi_ref[rows, hl] + _dot(w, xi_ref[:, hl].astype(BF16))
            on = acc * lax.rsqrt(jnp.mean(acc * acc, axis=-1, keepdims=True) + NORM_EPS)
            xg = xg_ref[rows, hl]
            o_ref[rows, hl] = (on * ng_ref[:, hl] * (xg * _sigmoid(xg))).astype(o_ref.dtype)


def _hgrn(proj, lb, ng, B, L, col0, tl=HG_ROW_TILE):
    N = proj.shape[0]
    W = lb.shape[1]
    hd = W // HG_HEADS
    tps = L // tl
    col = lambda c: pl.BlockSpec((tl, W), lambda b, t: (b * tps + t, col0 + c))
    return pl.pallas_call(
        _hgrn_kernel,
        out_shape=jax.ShapeDtypeStruct((N, W), BF16),
        grid=(B, tps),
        in_specs=[col(0), col(1), col(2), col(3), _resident((1, W)), _resident((1, W))],
        out_specs=pl.BlockSpec((tl, W), lambda b, t: (b * tps + t, 0)),
        scratch_shapes=[pltpu.VMEM((HG_HEADS, hd, hd), F32),
                        pltpu.VMEM((tl, W), BF16),
                        pltpu.VMEM((tl, W), BF16),
                        pltpu.VMEM((tl, W), F32),
                        pltpu.VMEM((tl, W), F32),
                        pltpu.VMEM((tl, W), F32),
                        pltpu.VMEM((tl, W), F32),
                        pltpu.VMEM((tl, W), F32),
                        pltpu.VMEM((tl, W), F32)],
        compiler_params=_cparams(2),
        name="hgrn2_mixer",
    )(proj, proj, proj, proj, lb, ng)


def _ffn_kernel(*refs, n_mix, tiles_per_seq, final_norm, tf):
    x_ref, xhalo_ref = refs[:2]
    mix_refs = refs[2:2 + n_mix]
    mixhalo_refs = refs[2 + n_mix:2 + 2 * n_mix]
    (wp_ref, g_ref, win_ref, cw_ref, cb_ref, wo_ref, gf_ref, o_ref,
     xn_ref, act_ref, a_ref, b_ref, h_ref) = refs[2 + 2 * n_mix:]
    i = pl.program_id(0)
    tm = x_ref.shape[0]
    H = FFN_HALO
    dff = wo_ref.shape[0]

    def mixed(r_ref, m_refs, rows):
        acc = r_ref[rows, :]
        k0 = 0
        for m_ref in m_refs:
            kw = m_ref.shape[1]
            acc = acc + _dot(m_ref[rows, :].astype(BF16), wp_ref[k0:k0 + kw, :])
            k0 += kw
        return acc

    hm = FFN_PART_ROWS
    parts = [(lo, lo + hm) for lo in range(0, tm, hm)]
    hn = _rms(mixed(xhalo_ref, mixhalo_refs, slice(0, H)), g_ref[...])
    xn_ref[0:H, :] = jnp.where(i % tiles_per_seq == 0, 0.0, hn).astype(BF16)
    for lo, hi in parts:
        h_ref[lo:hi, :] = mixed(x_ref, mix_refs, slice(lo, hi))
        xn_ref[H + lo:H + hi, :] = _rms(h_ref[lo:hi, :], g_ref[...]).astype(BF16)

    def conv(s_ref, c0, lo, hi):
        cols = slice(c0, c0 + tf)
        out = cb_ref[:, cols] + s_ref[H + lo:H + hi, :] * cw_ref[CONV_W - 1:CONV_W, cols]
        for t in range(1, CONV_W):
            out = out + s_ref[H + lo - t:H + hi - t, :] * cw_ref[CONV_W - 1 - t:CONV_W - t, cols]
        return out

    def up(j):
        for lo, hi in parts:
            strip = slice(0 if lo == 0 else H + lo, H + hi)
            a_ref[j % 2, strip, :] = _dot(xn_ref[strip, :], win_ref[:, j * tf:(j + 1) * tf])
            b_ref[j % 2, strip, :] = _dot(xn_ref[strip, :], win_ref[:, dff + j * tf:dff + (j + 1) * tf])

    def activate(j):
        for lo, hi in parts:
            a = conv(a_ref.at[j % 2], j * tf, lo, hi)
            b = conv(b_ref.at[j % 2], dff + j * tf, lo, hi)
            act_ref[lo:hi, j * tf:(j + 1) * tf] = (a * _sigmoid(a) * b).astype(BF16)

    nj = dff // tf
    for step in range(nj + 1):
        if step < nj:
            up(step)
        if step >= 1:
            activate(step - 1)

    for lo, hi in parts:
        y = h_ref[lo:hi, :] + _dot(act_ref[lo:hi, :], wo_ref[...])
        if final_norm:
            y = _rms(y, gf_ref[...])
        o_ref[lo:hi, :] = y


def _mix_ffn(x, mix_list, w_proj_bf, layer, g, w_in_bf, conv_w, conv_b, w_out_bf, g_final, L, final_norm,
             tm=FFN_ROW_TILE, tf=FFN_COL_TILE):
    N, D = x.shape
    dff = w_out_bf.shape[1]
    H = FFN_HALO
    main = lambda w: pl.BlockSpec((tm, w), lambda i: (i, 0))
    halo = lambda w: pl.BlockSpec((H, w), lambda i: (jnp.maximum(i * (tm // H) - 1, 0), 0))
    kern = functools.partial(_ffn_kernel, n_mix=len(mix_list), tiles_per_seq=L // tm,
                             final_norm=final_norm, tf=tf)
    return pl.pallas_call(
        kern,
        out_shape=jax.ShapeDtypeStruct((N, D), F32),
        grid=(N // tm,),
        in_specs=[main(D), halo(D)]
                 + [main(m.shape[1]) for m in mix_list] + [halo(m.shape[1]) for m in mix_list]
                 + [_resident(w_proj_bf.shape)]
                 + [_resident_layer(a.shape, layer) for a in (g, w_in_bf, conv_w, conv_b, w_out_bf)]
                 + [_resident((1, D))],
        out_specs=pl.BlockSpec((tm, D), lambda i: (i, 0)),
        scratch_shapes=[pltpu.VMEM((tm + H, D), BF16),
                        pltpu.VMEM((tm, dff), BF16),
                        pltpu.VMEM((2, tm + H, tf), F32),
                        pltpu.VMEM((2, tm + H, tf), F32),
                        pltpu.VMEM((tm, D), F32)],
        compiler_params=_cparams(1),
        name="mix_conv_ffn",
    )(x, x, *mix_list, *mix_list, w_proj_bf, g, w_in_bf, conv_w, conv_b, w_out_bf, g_final)


def _rot_lane_layout():
    half = ROT_DIM // 2
    E = ATT_HEAD_DIM
    lay = []
    for base in (0, half):
        lay += [(0, base + i) for i in range(half)] + [(1, base + i) for i in range(half)]
        lay += [(base // half, d) for d in range(ROT_DIM, E)]
    return lay


def _attn_kernel(q_ref, k_ref, v_ref, o_ref, m_a_ref, m_b_ref, acc_a_ref, acc_b_ref, tmp_ref,
                 qs_ref, ka_ref, kb_ref, va_ref, vb_ref):
    grp = pl.program_id(2)
    L, lw = q_ref.shape
    T = ATT_BLOCK
    E = ATT_HEAD_DIM
    n_groups = len(ATT_DILATIONS)

    lane1 = lax.broadcasted_iota(jnp.int32, (1, lw), 1)
    head_of_lane = np.array([h for h, _ in _rot_lane_layout()])
    edges = np.flatnonzero(np.diff(np.concatenate([[1], head_of_lane, [1]])))
    in_a_qk = jnp.zeros((1, lw), jnp.bool_)
    for lo, hi in zip(edges[0::2], edges[1::2]):
        in_a_qk = in_a_qk | ((lane1 >= int(lo)) & (lane1 < int(hi)))
    in_a_v = lane1 < E
    qi = lax.broadcasted_iota(jnp.int32, (T, T), 0)
    kj = lax.broadcasted_iota(jnp.int32, (T, T), 1)

    S = ATT_DILATIONS[1]
    assert ATT_DILATIONS == (1, S, S * S) and T % S == 0
    Ls = L // S

    def by_s_rows(dil, r, n):
        if dil == S:
            return slice(r * Ls + n * T, r * Ls + (n + 1) * T)
        return pl.ds((r % S) * Ls + r // S, T, stride=S)

    def run_group(gi, dil):
        M = L // dil
        nblk = M // T

        def residue_major(x_ref, emit):
            if dil == S * S:
                for r4 in range(S):
                    tmp_ref[r4 * Ls:(r4 + 1) * Ls, :] = x_ref[pl.ds(r4, Ls, stride=S), :]
            for r in range(dil):
                if dil == 1:
                    xr = x_ref[...]
                elif dil == S:
                    xr = x_ref[pl.ds(r, M, stride=S), :]
                else:
                    xr = tmp_ref[by_s_rows(dil, r, 0), :]
                emit(slice(r * M, (r + 1) * M), xr)

        def emit_q(dst, xr):
            qs_ref[dst, :] = (xr * (E ** -0.5 * LOG2E)).astype(BF16)

        def emit_k(dst, xr):
            ka_ref[dst, :] = jnp.where(in_a_qk, xr, 0.0).astype(BF16)
            kb_ref[dst, :] = jnp.where(in_a_qk, 0.0, xr).astype(BF16)

        def emit_v(dst, xr):
            va_ref[dst, :] = jnp.where(in_a_v, xr, 1.0).astype(BF16)
            vb_ref[dst, :] = jnp.where(in_a_v, 1.0, xr).astype(BF16)

        if dil == 1:
            for n in range(nblk):
                for j in range(S):
                    emit_q(slice(n * T + j * (T // S), n * T + (j + 1) * (T // S)),
                           q_ref[pl.ds(n * T + j, T // S, stride=S), :])
            q_pos = S * (qi % (T // S)) + qi // (T // S)
        else:
            residue_major(q_ref, emit_q)
            q_pos = qi
        cur_ok = kj <= q_pos
        both_ok = jnp.concatenate([kj >= q_pos, cur_ok], axis=1)
        residue_major(k_ref, emit_k)
        residue_major(v_ref, emit_v)

        k_refs, v_refs = (ka_ref, kb_ref), (va_ref, vb_ref)
        m_refs, acc_refs = (m_a_ref, m_b_ref), (acc_a_ref, acc_b_ref)
        blocks = [(r, n) for r in range(dil) for n in range(nblk)]

        def load_state(ref, r, n):
            return ref[by_s_rows(dil, r, n), :]

        def store_state(ref, r, n, val):
            if dil == 1:
                for j in range(S):
                    ref[j * Ls + (T // S) * n:j * Ls + (T // S) * (n + 1), :] = (
                        val[j * (T // S):(j + 1) * (T // S), :])
            else:
                ref[by_s_rows(dil, r, n), :] = val

        def kv_rows(r, n):
            c0 = r * M + n * T
            return slice(c0 - T, c0 + T) if n > 0 else slice(c0, c0 + T)

        def scores(r, n, hh):
            c0 = r * M + n * T
            return _dot_nt(qs_ref[c0:c0 + T, :], k_refs[hh][kv_rows(r, n), :])

        def softmax(r, n, hh, s):
            s = jnp.where(both_ok if n > 0 else cur_ok, s, NEG_BIG)
            m = jnp.broadcast_to(jnp.max(s, axis=-1, keepdims=True), (T, lw))
            alpha = None
            if gi > 0:
                m_old = load_state(m_refs[hh], r, n)
                m = jnp.maximum(m, m_old)
                alpha = jnp.exp2(m_old - m)
            if gi < n_groups - 1:
                store_state(m_refs[hh], r, n, m)
            if n > 0:
                m = jnp.concatenate([m, m], axis=1)
            return jnp.exp2(s - m).astype(BF16), alpha

        pv_first = {}

        def values(r, n, hh, p, alpha):
            pv = _dot(p, v_refs[hh][kv_rows(r, n), :])
            if gi > 0:
                pv = alpha * load_state(acc_refs[hh], r, n) + pv
            if gi < n_groups - 1:
                store_state(acc_refs[hh], r, n, pv)
            elif hh == 0:
                pv_first[(r, n)] = pv / pltpu.roll(pv, E, axis=1)
            else:
                tmp_ref[by_s_rows(dil, r, n), :] = jnp.where(
                    in_a_v, pv_first.pop((r, n)), pv / pltpu.roll(pv, E, axis=1))

        s_val, p_val = {}, {}
        for step in range(len(blocks) + 3):
            if step < len(blocks):
                s_val[step] = [scores(*blocks[step], hh) for hh in range(2)]
            if 0 <= step - 1 < len(blocks):
                s_pair = s_val.pop(step - 1)
                p_val[step - 1] = [softmax(*blocks[step - 1], hh, s_pair[hh]) for hh in range(2)]
            if 0 <= step - 3 < len(blocks):
                p_pair = p_val.pop(step - 3)
                for hh in range(2):
                    values(*blocks[step - 3], hh, *p_pair[hh])

        if gi == n_groups - 1:
            for r4 in range(S):
                o_ref[pl.ds(r4, Ls, stride=S), :] = tmp_ref[r4 * Ls:(r4 + 1) * Ls, :]

    for gi, dil in enumerate(ATT_DILATIONS):
        pl.when(grp == gi)(functools.partial(run_group, gi, dil))


def _attn(qkv, B, L, gw):
    N = qkv.shape[0]
    lw = 2 * ATT_HEAD_DIM
    npair = gw // lw
    ng = len(ATT_DILATIONS)

    def spec(kind):
        return pl.BlockSpec((L, lw), lambda b, p, g: (b, kind * ng * npair + g * npair + p))

    return pl.pallas_call(
        _attn_kernel,
        out_shape=jax.ShapeDtypeStruct((N, gw), F32),
        grid=(B, npair, ng),
        in_specs=[spec(0), spec(1), spec(2)],
        out_specs=pl.BlockSpec((L, lw), lambda b, p, g: (b, p)),
        scratch_shapes=[pltpu.VMEM((L, lw), F32)] * 4
                       + [pltpu.VMEM((L, lw), F32)]
                       + [pltpu.VMEM((L, lw), BF16)] * 5,
        compiler_params=_cparams(3),
        name="dilated_attention",
    )(qkv, qkv, qkv)


def _s5_params(A_re, A_im, log_dt, B_re, B_im, C_re, C_im, Dd):
    G, P = A_re.shape
    dt = jnp.exp(log_dt)[:, None]
    mag = jnp.exp(A_re * dt)
    ab_re = mag * jnp.cos(A_im * dt)
    ab_im = mag * jnp.sin(A_im * dt)
    den = A_re * A_re + A_im * A_im
    nr, ni = ab_re - 1.0, ab_im
    c_re = (nr * A_re + ni * A_im) / den
    c_im = (ni * A_re - nr * A_im) / den
    Bb_re = c_re[..., None] * B_re - c_im[..., None] * B_im
    Bb_im = c_re[..., None] * B_im + c_im[..., None] * B_re
    hg = G // 2
    eye = jnp.eye(hg, dtype=F32)

    def bd_in(m):
        return jnp.einsum('gpc,gh->gchp', m, eye).reshape(hg * S5_GROUP, hg * P)

    def bd_out(m):
        return jnp.einsum('gcp,gh->gphc', m, eye).reshape(hg * P, hg * S5_GROUP)

    bds, cds = [], []
    for k in range(2):
        sl = slice(k * hg, (k + 1) * hg)
        bds.append(jnp.concatenate([bd_in(Bb_re[sl]), bd_in(Bb_im[sl])], axis=1).astype(BF16))
        cds.append(jnp.concatenate([bd_out(C_re[sl]), -bd_out(C_im[sl])], axis=0).astype(BF16))
    return bds[0], bds[1], ab_re.reshape(1, G * P), ab_im.reshape(1, G * P), cds[0], cds[1], Dd.reshape(1, -1)


ROPE_COLS = 32


def _rope_tables(positions):
    half = ROT_DIM // 2
    inv_freq = ROPE_THETA ** (-jnp.arange(half, dtype=F32) * 2.0 / ROT_DIM)
    ang = positions.astype(F32).reshape(-1, 1) * inv_freq
    n = ang.shape[0]
    cs = jnp.concatenate([jnp.cos(ang), jnp.sin(ang), jnp.ones((n, 1), F32),
                          jnp.zeros((n, ROPE_COLS - 2 * half - 1), F32)], axis=1)
    ex = np.zeros((ROPE_COLS, 2 * LANES), np.float32)
    for lane, (_, d) in enumerate(_rot_lane_layout()):
        if d < half:
            ex[d, lane] = 1.0
            ex[half + d, LANES + lane] = -1.0
        elif d < ROT_DIM:
            ex[d - half, lane] = 1.0
            ex[half + d - half, LANES + lane] = 1.0
        else:
            ex[2 * half, lane] = 1.0
    return cs, jnp.asarray(ex, dtype=BF16)


def _qk_lane_layout(w):
    rows, n_cols = w.shape
    E = ATT_HEAD_DIM
    wp = w.reshape(rows, n_cols // LANES, 2, E)
    lay = _rot_lane_layout()
    pieces, start = [], 0
    for l in range(1, LANES + 1):
        if l == LANES or lay[l][0] != lay[start][0] or lay[l][1] != lay[l - 1][1] + 1:
            h, d0 = lay[start]
            pieces.append(wp[:, :, h, d0:d0 + (l - start)])
            start = l
    return jnp.concatenate(pieces, axis=-1).reshape(rows, n_cols)


def kernel(x, positions, norm_mix, norm_ffn, norm_final, mix_w_in, mix_w_out, s5_A_re, s5_A_im, s5_log_dt, s5_B_re, s5_B_im, s5_C_re, s5_C_im, s5_D, s5_glu_w, s5_glu_b, hgrn_gamma, hgrn_norm, att_w_qkv, att_w_o, ffn_w_in, ffn_conv_w, ffn_conv_b, ffn_w_out):
    B, L, D = x.shape
    N = B * L
    x2 = x.reshape(N, D)
    tm = PROJ_ROW_TILE
    s5w = s5_A_re.shape[1] * S5_GROUP
    hgw = hgrn_norm.shape[1]

    proj = _norm_proj(x2, norm_mix[0:1], mix_w_in[0].astype(BF16), tm, s5w, name="inproj")
    s5p = _s5_params(s5_A_re[0], s5_A_im[0], s5_log_dt[0], s5_B_re[0], s5_B_im[0],
                     s5_C_re[0], s5_C_im[0], s5_D[0])
    oa = _s5(proj.reshape(B, L, -1), *s5p, s5_glu_w[0].astype(BF16), s5_glu_b[0:1], s5w)
    lb_all = jnp.cumsum(jax.nn.softmax(hgrn_gamma.astype(F32), axis=0), axis=0)
    ob = _hgrn(proj, lb_all[0:1], hgrn_norm[0:1], B, L, col0=s5w // hgw)
    ffn_args = (norm_ffn[:, None, :], ffn_w_in.astype(BF16), ffn_conv_w, ffn_conv_b[:, None, :],
                ffn_w_out.astype(BF16), norm_final.reshape(1, D), L)
    h = _mix_ffn(x2, [oa.reshape(N, s5w), ob], mix_w_out[0].astype(BF16), 0, *ffn_args, final_norm=False)

    gw = att_w_o.shape[1]
    n_rot_cols = 2 * len(ATT_DILATIONS) * gw
    w_qkv = att_w_qkv[0].astype(BF16)
    w_qkv = jnp.concatenate([_qk_lane_layout(w_qkv[:, :n_rot_cols]), w_qkv[:, n_rot_cols:]], axis=1)
    qkv = _norm_proj(h, norm_mix[1:2], w_qkv, tm, gw, rot=_rope_tables(positions),
                     n_rot_cols=n_rot_cols, name="qkv_rotary")
    ao = _attn(qkv, B, L, gw)
    h = _mix_ffn(h, [ao], att_w_o[0].astype(BF16), 1, *ffn_args, final_norm=True)
    return h.reshape(B, L, D)
```

```python
import functools
import math

import numpy as np
import jax
import jax.numpy as jnp
from jax import lax
from jax.experimental import pallas as pl
from jax.experimental.pallas import tpu as pltpu

F32 = jnp.float32
BF16 = jnp.bfloat16

NORM_EPS = 1e-6
S5_GROUP = 16
S5_STATE = 64
HG_HEADS = 4
HG_BLOCK = 128
HG_LAG_BLOCK = 4
ATT_HEAD_DIM = 64
ATT_BLOCK = 128
ATT_DILATIONS = (1, 4, 16)
ROT_DIM = 16
ROPE_THETA = 500000.0
CONV_W = 3
NEG_BIG = -1e30
LOG2E = math.log2(math.e)
LANES = 128

PROJ_ROW_TILE = 512
FFN_ROW_TILE = 1024
FFN_COL_TILE = 256
FFN_PART_ROWS = 256
FFN_HALO = 16
S5_TIME_TILE = 64
S5_LANE_CHUNK = 1024
HG_ROW_TILE = 128
HG_ROW_CHUNK = 64

VMEM_PHYSICAL = 64 * 1024 * 1024
VMEM_LIMIT = VMEM_PHYSICAL - 4 * 1024 * 1024


def _cparams(n_axes):
    return pltpu.CompilerParams(dimension_semantics=("arbitrary",) * n_axes,
                                vmem_limit_bytes=VMEM_LIMIT)


def _resident(shape):
    return pl.BlockSpec(shape, lambda *_: (0,) * len(shape), pipeline_mode=pl.Buffered(1))


def _resident_layer(shape, layer):
    return pl.BlockSpec((None,) + tuple(shape[1:]), lambda *_: (layer,) + (0,) * (len(shape) - 1),
                        pipeline_mode=pl.Buffered(1))


def _rms(x, g):
    return x * lax.rsqrt(jnp.mean(x * x, axis=-1, keepdims=True) + NORM_EPS) * g


def _sigmoid(x):
    return 1.0 / (1.0 + jnp.exp(-x))


def _dot(a, b):
    return jnp.dot(a, b, preferred_element_type=F32)


def _dot_nt(a, b):
    return lax.dot_general(a, b, (((1,), (1,)), ((), ())), preferred_element_type=F32)


def _dot_tn(a, b):
    return lax.dot_general(a, b, (((0,), (0,)), ((), ())), preferred_element_type=F32)


def _norm_proj_kernel(x_ref, g_ref, w_ref, *rest, tn, n_rot_tiles, n_cast):
    rest = list(rest)
    xn_ref = rest.pop()
    cast_out = [rest.pop() for _ in range(n_cast)][::-1]
    o_ref = rest.pop()
    cast_in = [rest.pop() for _ in range(n_cast)][::-1]
    for src, dst in zip(cast_in, cast_out):
        dst[...] = src[...].astype(BF16)
    if n_rot_tiles:
        cs_ref, ex_ref = rest
        cs = cs_ref[...]
        cs_hi = cs.astype(BF16)
        cs_lo = (cs - cs_hi.astype(F32)).astype(BF16)
        tab = _dot(cs_hi, ex_ref[...]) + _dot(cs_lo, ex_ref[...])
        c = tab[:, :LANES]
        s = tab[:, LANES:]
    tm = x_ref.shape[0]
    parts = [slice(0, tm // 2), slice(tm // 2, tm)]
    for rows in parts:
        xn_ref[rows, :] = _rms(x_ref[rows, :], g_ref[...]).astype(BF16)
    for j in range(w_ref.shape[1] // tn):
        for rows in parts:
            y = _dot(xn_ref[rows, :], w_ref[:, j * tn:(j + 1) * tn])
            if j < n_rot_tiles:
                for k in range(tn // LANES):
                    t = y[:, k * LANES:(k + 1) * LANES]
                    o_ref[rows, j * tn + k * LANES:j * tn + (k + 1) * LANES] = (
                        t * c[rows, :] + pltpu.roll(t, LANES // 2, axis=1) * s[rows, :])
            else:
                o_ref[rows, j * tn:(j + 1) * tn] = y


def _norm_proj(x2, g, w_bf, tm, tn, rot=None, n_rot_cols=0, cast=(), name="norm_proj"):
    N, D = x2.shape
    ncols = w_bf.shape[1]
    steps = N // tm
    in_specs = [pl.BlockSpec((tm, D), lambda i: (i, 0)), _resident((1, D)), _resident(w_bf.shape)]
    args = [x2, g, w_bf]
    if rot is not None:
        cs, expand = rot
        in_specs += [pl.BlockSpec((tm, cs.shape[1]), lambda i: (i, 0)), _resident(expand.shape)]
        args += [cs, expand]
    out_shape = [jax.ShapeDtypeStruct((N, ncols), F32)]
    out_specs = [pl.BlockSpec((tm, ncols), lambda i: (i, 0))]
    for arr, layer, rows in cast:
        _, r_total, cols = arr.shape
        last = r_total // rows - 1
        assert r_total % rows == 0 and last < steps
        in_specs.append(pl.BlockSpec((None, rows, cols),
                                     lambda i, layer=layer, last=last: (layer, jnp.minimum(i, last), 0)))
        args.append(arr)
        out_shape.append(jax.ShapeDtypeStruct((r_total, cols), BF16))
        out_specs.append(pl.BlockSpec((rows, cols), lambda i, last=last: (jnp.minimum(i, last), 0)))
    return pl.pallas_call(
        functools.partial(_norm_proj_kernel, tn=tn, n_rot_tiles=n_rot_cols // tn, n_cast=len(cast)),
        out_shape=tuple(out_shape),
        grid=(steps,),
        in_specs=in_specs,
        out_specs=tuple(out_specs),
        scratch_shapes=[pltpu.VMEM((tm, D), BF16)],
        compiler_params=_cparams(1),
        name=name,
    )(*args)


def _s5_kernel(u_ref, bd0_ref, bd1_ref, ar_ref, ai_ref, c0_ref, c1_ref, d_ref, gw_ref, gb_ref,
               o_ref, x0_ref, x1_ref, st_ref, pf_ref, pb_ref, *, lane_chunk):
    nb, ts, W = u_ref.shape
    tm = nb * ts
    half = x0_ref.shape[1] // 2

    @pl.when(pl.program_id(0) == 0)
    def _():
        st_ref[...] = jnp.zeros_like(st_ref)
        r = lax.broadcasted_iota(jnp.int32, (tm, tm), 0)
        c = lax.broadcasted_iota(jnp.int32, (tm, tm), 1)
        pf_ref[...] = jnp.where(c == (r % nb) * ts + r // nb, 1.0, 0.0).astype(BF16)
        pb_ref[...] = jnp.where(r == (c % nb) * ts + c // nb, 1.0, 0.0).astype(BF16)

    u_bm = u_ref[...].reshape(tm, W)
    u_hi = u_bm.astype(BF16)
    u_lo = (u_bm - u_hi.astype(F32)).astype(BF16)
    ut_hi = _dot(pf_ref[...], u_hi)
    u = ut_hi + _dot(pf_ref[...], u_lo)
    ub = ut_hi.astype(BF16)
    ku = W // 2
    x0_ref[...] = _dot(ub[:, :ku], bd0_ref[...])
    x1_ref[...] = _dot(ub[:, ku:], bd1_ref[...])

    def scan(k, x_ref):
        for c in range(half // lane_chunk):
            lo = c * lane_chunk
            re = slice(lo, lo + lane_chunk)
            im = slice(half + lo, half + lo + lane_chunk)
            a_re = jnp.broadcast_to(ar_ref[:, k * half + lo:k * half + lo + lane_chunk], (nb, lane_chunk))
            a_im = jnp.broadcast_to(ai_ref[:, k * half + lo:k * half + lo + lane_chunk], (nb, lane_chunk))
            xr, xi = st_ref[2 * k, :, re], st_ref[2 * k + 1, :, re]
            for s in range(ts):
                rows = slice(s * nb, (s + 1) * nb)
                xr, xi = (a_re * xr - a_im * xi + x_ref[rows, re],
                          a_re * xi + a_im * xr + x_ref[rows, im])
                x_ref[rows, re] = xr
                x_ref[rows, im] = xi
            st_ref[2 * k, :, re] = xr
            st_ref[2 * k + 1, :, re] = xi

    scan(0, x0_ref)
    y0 = _dot(x0_ref[...].astype(BF16), c0_ref[...])
    scan(1, x1_ref)
    y1 = _dot(x1_ref[...].astype(BF16), c1_ref[...])
    y = jnp.concatenate([y0, y1], axis=1) + d_ref[...] * u
    z = 0.5 * y * (1.0 + jnp.tanh(math.sqrt(2.0 / math.pi) * (y + 0.044715 * (y * y * y))))
    gate = _sigmoid(_dot(z.astype(BF16), gw_ref[...]) + gb_ref[...])
    o_tm = (z * gate).astype(BF16)
    o_ref[...] = _dot(pb_ref[...], o_tm).astype(BF16).reshape(nb, ts, W)


def _s5(proj3, bd0, bd1, ar, ai, c0, c1, dd, gw, gb, W, ts=S5_TIME_TILE):
    B, L, _ = proj3.shape
    S2 = bd0.shape[1]
    tm = B * ts
    kern = functools.partial(_s5_kernel, lane_chunk=S5_LANE_CHUNK)
    return pl.pallas_call(
        kern,
        out_shape=jax.ShapeDtypeStruct((B, L, W), BF16),
        grid=(L // ts,),
        in_specs=[pl.BlockSpec((B, ts, W), lambda t: (0, t, 0))]
                 + [_resident(a.shape) for a in (bd0, bd1, ar, ai, c0, c1, dd, gw, gb)],
        out_specs=pl.BlockSpec((B, ts, W), lambda t: (0, t, 0)),
        scratch_shapes=[pltpu.VMEM((tm, S2), F32), pltpu.VMEM((tm, S2), F32),
                        pltpu.VMEM((4, B, S2 // 2), F32),
                        pltpu.VMEM((tm, tm), BF16), pltpu.VMEM((tm, tm), BF16)],
        compiler_params=_cparams(1),
        name="s5_mixer",
    )(proj3, bd0, bd1, ar, ai, c0, c1, dd, gw, gb)


def _hgrn_kernel(xq_ref, xf_ref, xi_ref, xg_ref, lb_ref, ng_ref, o_ref,
                 st_ref, qe_ref, ke_ref, eb_ref, q_ref, k_ref, f_ref, b_ref, oi_ref):
    tl, W = xq_ref.shape
    C = HG_BLOCK
    hd = W // HG_HEADS
    nblk = tl // C

    Cl = HG_LAG_BLOCK
    level_sizes = []
    while (2 * Cl) << len(level_sizes) <= C:
        level_sizes.append((2 * Cl) << len(level_sizes))

    @pl.when(pl.program_id(1) == 0)
    def _():
        st_ref[...] = jnp.zeros_like(st_ref)

    row = lax.broadcasted_iota(jnp.int32, (tl, tl), 0)
    col = lax.broadcasted_iota(jnp.int32, (tl, tl), 1)
    same = (row // C) == (col // C)
    t_cum = jnp.where(same & (col <= row), 1.0, 0.0).astype(BF16)
    t_blk = jnp.where(same, 1.0, 0.0).astype(BF16)

    xq = xq_ref[...]
    q = xq * _sigmoid(xq)
    lb = lb_ref[...]
    f = lb + (1.0 - lb) * _sigmoid(xf_ref[...])
    kk = 1.0 - f
    g2 = jnp.log(f) * LOG2E

    g_hi = g2.astype(BF16)
    g_lo = (g2 - g_hi.astype(F32)).astype(BF16)
    b = _dot(t_cum, g_hi) + _dot(t_cum, g_lo)
    bl = _dot(t_blk, g_hi) + _dot(t_blk, g_lo)

    q_ref[...] = q
    k_ref[...] = kk
    f_ref[...] = f
    b_ref[...] = b
    qe_ref[...] = (q * jnp.exp2(b)).astype(BF16)
    ke_ref[...] = (kk * jnp.exp2(bl - b)).astype(BF16)
    eb_ref[...] = jnp.exp2(bl)

    heads = [slice(h * hd, (h + 1) * hd) for h in range(HG_HEADS)]
    st = [st_ref[h] for h in range(HG_HEADS)]
    for i in range(nblk):
        rows = slice(i * C, (i + 1) * C)
        for h, hl in enumerate(heads):
            oi_ref[rows, hl] = _dot_nt(qe_ref[rows, hl], st[h].astype(BF16))
            vb = xi_ref[rows, hl].astype(BF16)
            st[h] = st[h] * eb_ref[i * C:i * C + 1, hl] + _dot_tn(vb, ke_ref[rows, hl])
    for h in range(HG_HEADS):
        st_ref[h] = st[h]

    levels = []
    for size in level_sizes:
        half = size // 2
        anchor = jnp.concatenate(
            [jnp.broadcast_to(b_ref[j * size + half - 1:j * size + half, :], (size, W))
             for j in range(tl // size)], axis=0)
        eh = jnp.exp2(-jnp.abs(b_ref[...] - anchor))
        pair_ok = ((row // size) == (col // size)) & ((row % size) >= half) & ((col % size) < half)
        levels.append(((q_ref[...] * eh).astype(BF16), (k_ref[...] * eh).astype(BF16), pair_ok))
    lvl = []
    for hl in heads:
        w = jnp.zeros((tl, tl), F32)
        for qh, kh, pair_ok in levels:
            w = jnp.where(pair_ok, _dot_nt(qh[:, hl], kh[:, hl]), w)
        lvl.append(w)

    rc = HG_ROW_CHUNK if tl % HG_ROW_CHUNK == 0 else tl
    r_i = lax.broadcasted_iota(jnp.int32, (rc, tl), 0)
    c_i = lax.broadcasted_iota(jnp.int32, (rc, tl), 1)
    for r0 in range(0, tl, rc):
        lag = jnp.where(((r_i + r0) // Cl) == (c_i // Cl), r_i + r0 - c_i, -1)
        rows = slice(r0, r0 + rc)
        p = [k_ref[rows, hl] for hl in heads]
        a_mat = [jnp.zeros((rc, tl), F32) for _ in heads]
        for d in range(Cl):
            for h, hl in enumerate(heads):
                if d > 0:
                    p[h] = f_ref[rows, hl] * pltpu.roll(p[h], 1, axis=0)
                a = jnp.sum(q_ref[rows, hl] * p[h], axis=-1, keepdims=True)
                a_mat[h] = jnp.where(lag == d, a, a_mat[h])
        for h, hl in enumerate(heads):
            w = (a_mat[h] + lvl[h][r0:r0 + rc, :]).astype(BF16)
            acc = oi_ref[rows, hl] + _dot(w, xi_ref[:, hl].astype(BF16))
            on = acc * lax.rsqrt(jnp.mean(acc * acc, axis=-1, keepdims=True) + NORM_EPS)
            xg = xg_ref[rows, hl]
            o_ref[rows, hl] = (on * ng_ref[:, hl] * (xg * _sigmoid(xg))).astype(o_ref.dtype)


def _hgrn(proj, lb, ng, B, L, col0, tl=HG_ROW_TILE):
    N = proj.shape[0]
    W = lb.shape[1]
    hd = W // HG_HEADS
    tps = L // tl
    col = lambda c: pl.BlockSpec((tl, W), lambda b, t: (b * tps + t, col0 + c))
    return pl.pallas_call(
        _hgrn_kernel,
        out_shape=jax.ShapeDtypeStruct((N, W), BF16),
        grid=(B, tps),
        in_specs=[col(0), col(1), col(2), col(3), _resident((1, W)), _resident((1, W))],
        out_specs=pl.BlockSpec((tl, W), lambda b, t: (b * tps + t, 0)),
        scratch_shapes=[pltpu.VMEM((HG_HEADS, hd, hd), F32),
                        pltpu.VMEM((tl, W), BF16),
                        pltpu.VMEM((tl, W), BF16),
                        pltpu.VMEM((tl, W), F32),
                        pltpu.VMEM((tl, W), F32),
                        pltpu.VMEM((tl, W), F32),
                        pltpu.VMEM((tl, W), F32),
                        pltpu.VMEM((tl, W), F32),
                        pltpu.VMEM((tl, W), F32)],
        compiler_params=_cparams(2),
        name="hgrn2_mixer",
    )(proj, proj, proj, proj, lb, ng)


def _ffn_kernel(*refs, n_mix, tiles_per_seq, final_norm, tf):
    x_ref, xhalo_ref = refs[:2]
    mix_refs = refs[2:2 + n_mix]
    mixhalo_refs = refs[2 + n_mix:2 + 2 * n_mix]
    (wp_ref, g_ref, win_ref, cw_ref, cb_ref, wo_ref, gf_ref, o_ref,
     xn_ref, act_ref, a_ref, b_ref, h_ref) = refs[2 + 2 * n_mix:]
    i = pl.program_id(0)
    tm = x_ref.shape[0]
    H = FFN_HALO
    dff = wo_ref.shape[0]

    def mixed(r_ref, m_refs, rows):
        acc = r_ref[rows, :]
        k0 = 0
        for m_ref in m_refs:
            kw = m_ref.shape[1]
            acc = acc + _dot(m_ref[rows, :].astype(BF16), wp_ref[k0:k0 + kw, :])
            k0 += kw
        return acc

    hm = FFN_PART_ROWS
    parts = [(lo, lo + hm) for lo in range(0, tm, hm)]
    hn = _rms(mixed(xhalo_ref, mixhalo_refs, slice(0, H)), g_ref[...])
    xn_ref[0:H, :] = jnp.where(i % tiles_per_seq == 0, 0.0, hn).astype(BF16)
    for lo, hi in parts:
        h_ref[lo:hi, :] = mixed(x_ref, mix_refs, slice(lo, hi))
        xn_ref[H + lo:H + hi, :] = _rms(h_ref[lo:hi, :], g_ref[...]).astype(BF16)

    def conv(s_ref, c0, lo, hi):
        cols = slice(c0, c0 + tf)
        out = cb_ref[:, cols] + s_ref[H + lo:H + hi, :] * cw_ref[CONV_W - 1:CONV_W, cols]
        for t in range(1, CONV_W):
            out = out + s_ref[H + lo - t:H + hi - t, :] * cw_ref[CONV_W - 1 - t:CONV_W - t, cols]
        return out

    def up(j):
        for lo, hi in parts:
            strip = slice(0 if lo == 0 else H + lo, H + hi)
            a_ref[j % 2, strip, :] = _dot(xn_ref[strip, :], win_ref[:, j * tf:(j + 1) * tf])
            b_ref[j % 2, strip, :] = _dot(xn_ref[strip, :], win_ref[:, dff + j * tf:dff + (j + 1) * tf])

    def activate(j):
        for lo, hi in parts:
            a = conv(a_ref.at[j % 2], j * tf, lo, hi)
            b = conv(b_ref.at[j % 2], dff + j * tf, lo, hi)
            act_ref[lo:hi, j * tf:(j + 1) * tf] = (a * _sigmoid(a) * b).astype(BF16)

    nj = dff // tf
    for step in range(nj + 1):
        if step < nj:
            up(step)
        if step >= 1:
            activate(step - 1)

    for lo, hi in parts:
        y = h_ref[lo:hi, :] + _dot(act_ref[lo:hi, :], wo_ref[...])
        if final_norm:
            y = _rms(y, gf_ref[...])
        o_ref[lo:hi, :] = y


def _mix_ffn(x, mix_list, w_proj_bf, layer, g, w_in_bf, conv_w, conv_b, w_out_bf, g_final, L, final_norm,
             tm=FFN_ROW_TILE, tf=FFN_COL_TILE):
    N, D = x.shape
    dff = w_out_bf.shape[0]
    H = FFN_HALO
    main = lambda w: pl.BlockSpec((tm, w), lambda i: (i, 0))
    halo = lambda w: pl.BlockSpec((H, w), lambda i: (jnp.maximum(i * (tm // H) - 1, 0), 0))
    kern = functools.partial(_ffn_kernel, n_mix=len(mix_list), tiles_per_seq=L // tm,
                             final_norm=final_norm, tf=tf)
    return pl.pallas_call(
        kern,
        out_shape=jax.ShapeDtypeStruct((N, D), F32),
        grid=(N // tm,),
        in_specs=[main(D), halo(D)]
                 + [main(m.shape[1]) for m in mix_list] + [halo(m.shape[1]) for m in mix_list]
                 + [_resident(w_proj_bf.shape), _resident_layer(g.shape, layer), _resident(w_in_bf.shape),
                    _resident_layer(conv_w.shape, layer), _resident_layer(conv_b.shape, layer),
                    _resident(w_out_bf.shape), _resident((1, D))],
        out_specs=pl.BlockSpec((tm, D), lambda i: (i, 0)),
        scratch_shapes=[pltpu.VMEM((tm + H, D), BF16),
                        pltpu.VMEM((tm, dff), BF16),
                        pltpu.VMEM((2, tm + H, tf), F32),
                        pltpu.VMEM((2, tm + H, tf), F32),
                        pltpu.VMEM((tm, D), F32)],
        compiler_params=_cparams(1),
        name="mix_conv_ffn",
    )(x, x, *mix_list, *mix_list, w_proj_bf, g, w_in_bf, conv_w, conv_b, w_out_bf, g_final)


def _rot_lane_layout():
    half = ROT_DIM // 2
    E = ATT_HEAD_DIM
    lay = []
    for base in (0, half):
        lay += [(0, base + i) for i in range(half)] + [(1, base + i) for i in range(half)]
        lay += [(base // half, d) for d in range(ROT_DIM, E)]
    return lay


def _attn_kernel(q_ref, k_ref, v_ref, o_ref, m_a_ref, m_b_ref, acc_a_ref, acc_b_ref, tmp_ref,
                 qs_ref, ka_ref, kb_ref, va_ref, vb_ref):
    grp = pl.program_id(2)
    L, lw = q_ref.shape
    T = ATT_BLOCK
    E = ATT_HEAD_DIM
    n_groups = len(ATT_DILATIONS)

    lane1 = lax.broadcasted_iota(jnp.int32, (1, lw), 1)
    head_of_lane = np.array([h for h, _ in _rot_lane_layout()])
    edges = np.flatnonzero(np.diff(np.concatenate([[1], head_of_lane, [1]])))
    in_a_qk = jnp.zeros((1, lw), jnp.bool_)
    for lo, hi in zip(edges[0::2], edges[1::2]):
        in_a_qk = in_a_qk | ((lane1 >= int(lo)) & (lane1 < int(hi)))
    in_a_v = lane1 < E
    qi = lax.broadcasted_iota(jnp.int32, (T, T), 0)
    kj = lax.broadcasted_iota(jnp.int32, (T, T), 1)

    S = ATT_DILATIONS[1]
    assert ATT_DILATIONS == (1, S, S * S) and T % S == 0
    Ls = L // S

    def by_s_rows(dil, r, n):
        if dil == S:
            return slice(r * Ls + n * T, r * Ls + (n + 1) * T)
        return pl.ds((r % S) * Ls + r // S, T, stride=S)

    def run_group(gi, dil):
        M = L // dil
        nblk = M // T

        def residue_major(x_ref, emit):
            if dil == S * S:
                for r4 in range(S):
                    tmp_ref[r4 * Ls:(r4 + 1) * Ls, :] = x_ref[pl.ds(r4, Ls, stride=S), :]
            for r in range(dil):
                if dil == 1:
                    xr = x_ref[...]
                elif dil == S:
                    xr = x_ref[pl.ds(r, M, stride=S), :]
                else:
                    xr = tmp_ref[by_s_rows(dil, r, 0), :]
                emit(slice(r * M, (r + 1) * M), xr)

        def emit_q(dst, xr):
            qs_ref[dst, :] = (xr * (E ** -0.5 * LOG2E)).astype(BF16)

        def emit_k(dst, xr):
            ka_ref[dst, :] = jnp.where(in_a_qk, xr, 0.0).astype(BF16)
            kb_ref[dst, :] = jnp.where(in_a_qk, 0.0, xr).astype(BF16)

        def emit_v(dst, xr):
            va_ref[dst, :] = jnp.where(in_a_v, xr, 1.0).astype(BF16)
            vb_ref[dst, :] = jnp.where(in_a_v, 1.0, xr).astype(BF16)

        if dil == 1:
            for n in range(nblk):
                for j in range(S):
                    emit_q(slice(n * T + j * (T // S), n * T + (j + 1) * (T // S)),
                           q_ref[pl.ds(n * T + j, T // S, stride=S), :])
            q_pos = S * (qi % (T // S)) + qi // (T // S)
        else:
            residue_major(q_ref, emit_q)
            q_pos = qi
        cur_ok = kj <= q_pos
        both_ok = jnp.concatenate([kj >= q_pos, cur_ok], axis=1)
        residue_major(k_ref, emit_k)
        residue_major(v_ref, emit_v)

        k_refs, v_refs = (ka_ref, kb_ref), (va_ref, vb_ref)
        m_refs, acc_refs = (m_a_ref, m_b_ref), (acc_a_ref, acc_b_ref)
        blocks = [(r, n) for r in range(dil) for n in range(nblk)]

        def load_state(ref, r, n):
            return ref[by_s_rows(dil, r, n), :]

        def store_state(ref, r, n, val):
            if dil == 1:
                for j in range(S):
                    ref[j * Ls + (T // S) * n:j * Ls + (T // S) * (n + 1), :] = (
                        val[j * (T // S):(j + 1) * (T // S), :])
            else:
                ref[by_s_rows(dil, r, n), :] = val

        def kv_rows(r, n):
            c0 = r * M + n * T
            return slice(c0 - T, c0 + T) if n > 0 else slice(c0, c0 + T)

        def scores(r, n, hh):
            c0 = r * M + n * T
            return _dot_nt(qs_ref[c0:c0 + T, :], k_refs[hh][kv_rows(r, n), :])

        def softmax(r, n, hh, s):
            s = jnp.where(both_ok if n > 0 else cur_ok, s, NEG_BIG)
            m = jnp.broadcast_to(jnp.max(s, axis=-1, keepdims=True), (T, lw))
            alpha = None
            if gi > 0:
                m_old = load_state(m_refs[hh], r, n)
                m = jnp.maximum(m, m_old)
                alpha = jnp.exp2(m_old - m)
            if gi < n_groups - 1:
                store_state(m_refs[hh], r, n, m)
            if n > 0:
                m = jnp.concatenate([m, m], axis=1)
            return jnp.exp2(s - m).astype(BF16), alpha

        pv_first = {}

        def values(r, n, hh, p, alpha):
            pv = _dot(p, v_refs[hh][kv_rows(r, n), :])
            if gi > 0:
                pv = alpha * load_state(acc_refs[hh], r, n) + pv
            if gi < n_groups - 1:
                store_state(acc_refs[hh], r, n, pv)
            elif hh == 0:
                pv_first[(r, n)] = pv / pltpu.roll(pv, E, axis=1)
            else:
                tmp_ref[by_s_rows(dil, r, n), :] = jnp.where(
                    in_a_v, pv_first.pop((r, n)), pv / pltpu.roll(pv, E, axis=1))

        s_val, p_val = {}, {}
        for step in range(len(blocks) + 3):
            if step < len(blocks):
                s_val[step] = [scores(*blocks[step], hh) for hh in range(2)]
            if 0 <= step - 1 < len(blocks):
                s_pair = s_val.pop(step - 1)
                p_val[step - 1] = [softmax(*blocks[step - 1], hh, s_pair[hh]) for hh in range(2)]
            if 0 <= step - 3 < len(blocks):
                p_pair = p_val.pop(step - 3)
                for hh in range(2):
                    values(*blocks[step - 3], hh, *p_pair[hh])

        if gi == n_groups - 1:
            for r4 in range(S):
                o_ref[pl.ds(r4, Ls, stride=S), :] = tmp_ref[r4 * Ls:(r4 + 1) * Ls, :]

    for gi, dil in enumerate(ATT_DILATIONS):
        pl.when(grp == gi)(functools.partial(run_group, gi, dil))


def _attn(qkv, B, L, gw):
    N = qkv.shape[0]
    lw = 2 * ATT_HEAD_DIM
    npair = gw // lw
    ng = len(ATT_DILATIONS)

    def spec(kind):
        return pl.BlockSpec((L, lw), lambda b, p, g: (b, kind * ng * npair + g * npair + p))

    return pl.pallas_call(
        _attn_kernel,
        out_shape=jax.ShapeDtypeStruct((N, gw), F32),
        grid=(B, npair, ng),
        in_specs=[spec(0), spec(1), spec(2)],
        out_specs=pl.BlockSpec((L, lw), lambda b, p, g: (b, p)),
        scratch_shapes=[pltpu.VMEM((L, lw), F32)] * 4
                       + [pltpu.VMEM((L, lw), F32)]
                       + [pltpu.VMEM((L, lw), BF16)] * 5,
        compiler_params=_cparams(3),
        name="dilated_attention",
    )(qkv, qkv, qkv)


def _s5_params(A_re, A_im, log_dt, B_re, B_im, C_re, C_im, Dd):
    G, P = A_re.shape
    dt = jnp.exp(log_dt)[:, None]
    mag = jnp.exp(A_re * dt)
    ab_re = mag * jnp.cos(A_im * dt)
    ab_im = mag * jnp.sin(A_im * dt)
    den = A_re * A_re + A_im * A_im
    nr, ni = ab_re - 1.0, ab_im
    c_re = (nr * A_re + ni * A_im) / den
    c_im = (ni * A_re - nr * A_im) / den
    Bb_re = c_re[..., None] * B_re - c_im[..., None] * B_im
    Bb_im = c_re[..., None] * B_im + c_im[..., None] * B_re
    hg = G // 2
    eye = jnp.eye(hg, dtype=F32)

    def bd_in(m):
        return jnp.einsum('gpc,gh->gchp', m, eye).reshape(hg * S5_GROUP, hg * P)

    def bd_out(m):
        return jnp.einsum('gcp,gh->gphc', m, eye).reshape(hg * P, hg * S5_GROUP)

    bds, cds = [], []
    for k in range(2):
        sl = slice(k * hg, (k + 1) * hg)
        bds.append(jnp.concatenate([bd_in(Bb_re[sl]), bd_in(Bb_im[sl])], axis=1).astype(BF16))
        cds.append(jnp.concatenate([bd_out(C_re[sl]), -bd_out(C_im[sl])], axis=0).astype(BF16))
    return bds[0], bds[1], ab_re.reshape(1, G * P), ab_im.reshape(1, G * P), cds[0], cds[1], Dd.reshape(1, -1)


ROPE_COLS = 32


def _rope_tables(positions):
    half = ROT_DIM // 2
    inv_freq = ROPE_THETA ** (-jnp.arange(half, dtype=F32) * 2.0 / ROT_DIM)
    ang = positions.astype(F32).reshape(-1, 1) * inv_freq
    n = ang.shape[0]
    cs = jnp.concatenate([jnp.cos(ang), jnp.sin(ang), jnp.ones((n, 1), F32),
                          jnp.zeros((n, ROPE_COLS - 2 * half - 1), F32)], axis=1)
    ex = np.zeros((ROPE_COLS, 2 * LANES), np.float32)
    for lane, (_, d) in enumerate(_rot_lane_layout()):
        if d < half:
            ex[d, lane] = 1.0
            ex[half + d, LANES + lane] = -1.0
        elif d < ROT_DIM:
            ex[d - half, lane] = 1.0
            ex[half + d - half, LANES + lane] = 1.0
        else:
            ex[2 * half, lane] = 1.0
    return cs, jnp.asarray(ex, dtype=BF16)


def _qk_lane_layout(w):
    rows, n_cols = w.shape
    E = ATT_HEAD_DIM
    wp = w.reshape(rows, n_cols // LANES, 2, E)
    lay = _rot_lane_layout()
    pieces, start = [], 0
    for l in range(1, LANES + 1):
        if l == LANES or lay[l][0] != lay[start][0] or lay[l][1] != lay[l - 1][1] + 1:
            h, d0 = lay[start]
            pieces.append(wp[:, :, h, d0:d0 + (l - start)])
            start = l
    return jnp.concatenate(pieces, axis=-1).reshape(rows, n_cols)


def kernel(x, positions, norm_mix, norm_ffn, norm_final, mix_w_in, mix_w_out, s5_A_re, s5_A_im, s5_log_dt, s5_B_re, s5_B_im, s5_C_re, s5_C_im, s5_D, s5_glu_w, s5_glu_b, hgrn_gamma, hgrn_norm, att_w_qkv, att_w_o, ffn_w_in, ffn_conv_w, ffn_conv_b, ffn_w_out):
    B, L, D = x.shape
    N = B * L
    x2 = x.reshape(N, D)
    tm = PROJ_ROW_TILE
    s5w = s5_A_re.shape[1] * S5_GROUP
    hgw = hgrn_norm.shape[1]

    steps = N // tm
    dff = ffn_w_out.shape[1]
    ffn_cast = lambda layer: ((ffn_w_in, layer, D // steps), (ffn_w_out, layer, FFN_COL_TILE))
    assert D % steps == 0 and dff % FFN_COL_TILE == 0
    g_ffn, conv_b = norm_ffn[:, None, :], ffn_conv_b[:, None, :]
    g_fin = norm_final.reshape(1, D)

    proj, w_in0, w_out0 = _norm_proj(x2, norm_mix[0:1], mix_w_in[0].astype(BF16), tm, s5w,
                                     cast=ffn_cast(0), name="inproj")
    s5p = _s5_params(s5_A_re[0], s5_A_im[0], s5_log_dt[0], s5_B_re[0], s5_B_im[0],
                     s5_C_re[0], s5_C_im[0], s5_D[0])
    oa = _s5(proj.reshape(B, L, -1), *s5p, s5_glu_w[0].astype(BF16), s5_glu_b[0:1], s5w)
    lb_all = jnp.cumsum(jax.nn.softmax(hgrn_gamma.astype(F32), axis=0), axis=0)
    ob = _hgrn(proj, lb_all[0:1], hgrn_norm[0:1], B, L, col0=s5w // hgw)
    h = _mix_ffn(x2, [oa.reshape(N, s5w), ob], mix_w_out[0].astype(BF16), 0, g_ffn, w_in0, ffn_conv_w,
                 conv_b, w_out0, g_fin, L, final_norm=False)

    gw = att_w_o.shape[1]
    n_rot_cols = 2 * len(ATT_DILATIONS) * gw
    w_qkv = att_w_qkv[0].astype(BF16)
    w_qkv = jnp.concatenate([_qk_lane_layout(w_qkv[:, :n_rot_cols]), w_qkv[:, n_rot_cols:]], axis=1)
    qkv, w_in1, w_out1 = _norm_proj(h, norm_mix[1:2], w_qkv, tm, gw, rot=_rope_tables(positions),
                                    n_rot_cols=n_rot_cols, cast=ffn_cast(1), name="qkv_rotary")
    ao = _attn(qkv, B, L, gw)
    h = _mix_ffn(h, [ao], att_w_o[0].astype(BF16), 1, g_ffn, w_in1, ffn_conv_w, conv_b, w_out1, g_fin, L,
                 final_norm=True)
    return h.reshape(B, L, D)
```

```python
import functools
import math

import numpy as np
import jax
import jax.numpy as jnp
from jax import lax
from jax.experimental import pallas as pl
from jax.experimental.pallas import tpu as pltpu

F32 = jnp.float32
BF16 = jnp.bfloat16

NORM_EPS = 1e-6
S5_GROUP = 16
S5_STATE = 64
HG_HEADS = 4
HG_BLOCK = 128
HG_LAG_BLOCK = 4
ATT_HEAD_DIM = 64
ATT_BLOCK = 128
ATT_DILATIONS = (1, 4, 16)
ROT_DIM = 16
ROPE_THETA = 500000.0
CONV_W = 3
NEG_BIG = -1e30
LOG2E = math.log2(math.e)
LANES = 128

PROJ_ROW_TILE = 512
FFN_ROW_TILE = 1024
FFN_COL_TILE = 256
FFN_PART_ROWS = 256
FFN_HALO = 16
S5_TIME_TILE = 64
S5_LANE_CHUNK = 1024
HG_ROW_TILE = 128
HG_ROW_CHUNK = 64

VMEM_PHYSICAL = 64 * 1024 * 1024
VMEM_LIMIT = VMEM_PHYSICAL - 4 * 1024 * 1024


def _cparams(n_axes):
    return pltpu.CompilerParams(dimension_semantics=("arbitrary",) * n_axes,
                                vmem_limit_bytes=VMEM_LIMIT)


def _resident(shape):
    return pl.BlockSpec(shape, lambda *_: (0,) * len(shape), pipeline_mode=pl.Buffered(1))


def _resident_layer(shape, layer):
    return pl.BlockSpec((None,) + tuple(shape[1:]), lambda *_: (layer,) + (0,) * (len(shape) - 1),
                        pipeline_mode=pl.Buffered(1))


def _rms(x, g):
    return x * lax.rsqrt(jnp.mean(x * x, axis=-1, keepdims=True) + NORM_EPS) * g


def _sigmoid(x):
    return 1.0 / (1.0 + jnp.exp(-x))


def _dot(a, b):
    return jnp.dot(a, b, preferred_element_type=F32)


def _dot_nt(a, b):
    return lax.dot_general(a, b, (((1,), (1,)), ((), ())), preferred_element_type=F32)


def _dot_tn(a, b):
    return lax.dot_general(a, b, (((0,), (0,)), ((), ())), preferred_element_type=F32)


def _norm_proj_kernel(x_ref, g_ref, w_ref, *rest, tn, n_rot_tiles, n_cast):
    rest = list(rest)
    xn_ref = rest.pop()
    cast_out = [rest.pop() for _ in range(n_cast)][::-1]
    o_ref = rest.pop()
    cast_in = [rest.pop() for _ in range(n_cast)][::-1]
    for src, dst in zip(cast_in, cast_out):
        dst[...] = src[...].astype(BF16)
    if n_rot_tiles:
        cs_ref, ex_ref = rest
        cs = cs_ref[...]
        cs_hi = cs.astype(BF16)
        cs_lo = (cs - cs_hi.astype(F32)).astype(BF16)
        tab = _dot(cs_hi, ex_ref[...]) + _dot(cs_lo, ex_ref[...])
        c = tab[:, :LANES]
        s = tab[:, LANES:]
    tm = x_ref.shape[0]
    parts = [slice(0, tm // 2), slice(tm // 2, tm)]
    for rows in parts:
        xn_ref[rows, :] = _rms(x_ref[rows, :], g_ref[...]).astype(BF16)
    for j in range(w_ref.shape[1] // tn):
        for rows in parts:
            y = _dot(xn_ref[rows, :], w_ref[:, j * tn:(j + 1) * tn])
            if j < n_rot_tiles:
                for k in range(tn // LANES):
                    t = y[:, k * LANES:(k + 1) * LANES]
                    o_ref[rows, j * tn + k * LANES:j * tn + (k + 1) * LANES] = (
                        t * c[rows, :] + pltpu.roll(t, LANES // 2, axis=1) * s[rows, :])
            else:
                o_ref[rows, j * tn:(j + 1) * tn] = y


def _cast_side_job(cast, steps):
    in_specs, args, out_shape, out_specs = [], [], [], []
    for arr, layer, rows in cast:
        _, r_total, cols = arr.shape
        last = r_total // rows - 1
        assert r_total % rows == 0 and last < steps
        in_specs.append(pl.BlockSpec((None, rows, cols),
                                     lambda i, layer=layer, last=last: (layer, jnp.minimum(i, last), 0)))
        args.append(arr)
        out_shape.append(jax.ShapeDtypeStruct((r_total, cols), BF16))
        out_specs.append(pl.BlockSpec((rows, cols), lambda i, last=last: (jnp.minimum(i, last), 0)))
    return in_specs, args, out_shape, out_specs


def _norm_proj(x2, g, w_bf, tm, tn, rot=None, n_rot_cols=0, cast=(), name="norm_proj"):
    N, D = x2.shape
    ncols = w_bf.shape[1]
    steps = N // tm
    in_specs = [pl.BlockSpec((tm, D), lambda i: (i, 0)), _resident((1, D)), _resident(w_bf.shape)]
    args = [x2, g, w_bf]
    if rot is not None:
        cs, expand = rot
        in_specs += [pl.BlockSpec((tm, cs.shape[1]), lambda i: (i, 0)), _resident(expand.shape)]
        args += [cs, expand]
    out_shape = [jax.ShapeDtypeStruct((N, ncols), F32)]
    out_specs = [pl.BlockSpec((tm, ncols), lambda i: (i, 0))]
    c_in, c_args, c_shape, c_out = _cast_side_job(cast, steps)
    in_specs += c_in
    args += c_args
    out_shape += c_shape
    out_specs += c_out
    return pl.pallas_call(
        functools.partial(_norm_proj_kernel, tn=tn, n_rot_tiles=n_rot_cols // tn, n_cast=len(cast)),
        out_shape=tuple(out_shape),
        grid=(steps,),
        in_specs=in_specs,
        out_specs=tuple(out_specs),
        scratch_shapes=[pltpu.VMEM((tm, D), BF16)],
        compiler_params=_cparams(1),
        name=name,
    )(*args)


def _s5_kernel(u_ref, bd0_ref, bd1_ref, ar_ref, ai_ref, c0_ref, c1_ref, d_ref, gw_ref, gb_ref,
               *rest, lane_chunk, n_cast):
    cast_in = rest[:n_cast]
    o_ref = rest[n_cast]
    cast_out = rest[n_cast + 1:2 * n_cast + 1]
    x0_ref, x1_ref, st_ref, pf_ref, pb_ref = rest[2 * n_cast + 1:]
    for src, dst in zip(cast_in, cast_out):
        dst[...] = src[...].astype(BF16)
    nb, ts, W = u_ref.shape
    tm = nb * ts
    half = x0_ref.shape[1] // 2

    @pl.when(pl.program_id(0) == 0)
    def _():
        st_ref[...] = jnp.zeros_like(st_ref)
        r = lax.broadcasted_iota(jnp.int32, (tm, tm), 0)
        c = lax.broadcasted_iota(jnp.int32, (tm, tm), 1)
        pf_ref[...] = jnp.where(c == (r % nb) * ts + r // nb, 1.0, 0.0).astype(BF16)
        pb_ref[...] = jnp.where(r == (c % nb) * ts + c // nb, 1.0, 0.0).astype(BF16)

    u_bm = u_ref[...].reshape(tm, W)
    u_hi = u_bm.astype(BF16)
    u_lo = (u_bm - u_hi.astype(F32)).astype(BF16)
    ut_hi = _dot(pf_ref[...], u_hi)
    u = ut_hi + _dot(pf_ref[...], u_lo)
    ub = ut_hi.astype(BF16)
    ku = W // 2
    x0_ref[...] = _dot(ub[:, :ku], bd0_ref[...])
    x1_ref[...] = _dot(ub[:, ku:], bd1_ref[...])

    def scan(k, x_ref):
        for c in range(half // lane_chunk):
            lo = c * lane_chunk
            re = slice(lo, lo + lane_chunk)
            im = slice(half + lo, half + lo + lane_chunk)
            a_re = jnp.broadcast_to(ar_ref[:, k * half + lo:k * half + lo + lane_chunk], (nb, lane_chunk))
            a_im = jnp.broadcast_to(ai_ref[:, k * half + lo:k * half + lo + lane_chunk], (nb, lane_chunk))
            xr, xi = st_ref[2 * k, :, re], st_ref[2 * k + 1, :, re]
            for s in range(ts):
                rows = slice(s * nb, (s + 1) * nb)
                xr, xi = (a_re * xr - a_im * xi + x_ref[rows, re],
                          a_re * xi + a_im * xr + x_ref[rows, im])
                x_ref[rows, re] = xr
                x_ref[rows, im] = xi
            st_ref[2 * k, :, re] = xr
            st_ref[2 * k + 1, :, re] = xi

    scan(0, x0_ref)
    y0 = _dot(x0_ref[...].astype(BF16), c0_ref[...])
    scan(1, x1_ref)
    y1 = _dot(x1_ref[...].astype(BF16), c1_ref[...])
    y = jnp.concatenate([y0, y1], axis=1) + d_ref[...] * u
    z = 0.5 * y * (1.0 + jnp.tanh(math.sqrt(2.0 / math.pi) * (y + 0.044715 * (y * y * y))))
    gate = _sigmoid(_dot(z.astype(BF16), gw_ref[...]) + gb_ref[...])
    o_tm = (z * gate).astype(BF16)
    o_ref[...] = _dot(pb_ref[...], o_tm).astype(BF16).reshape(nb, ts, W)


def _s5(proj3, bd0, bd1, ar, ai, c0, c1, dd, gw, gb, W, cast=(), ts=S5_TIME_TILE):
    B, L, _ = proj3.shape
    S2 = bd0.shape[1]
    tm = B * ts
    c_in, c_args, c_shape, c_out = _cast_side_job(cast, L // ts)
    kern = functools.partial(_s5_kernel, lane_chunk=S5_LANE_CHUNK, n_cast=len(cast))
    return pl.pallas_call(
        kern,
        out_shape=tuple([jax.ShapeDtypeStruct((B, L, W), BF16)] + c_shape),
        grid=(L // ts,),
        in_specs=[pl.BlockSpec((B, ts, W), lambda t: (0, t, 0))]
                 + [_resident(a.shape) for a in (bd0, bd1, ar, ai, c0, c1, dd, gw, gb)] + c_in,
        out_specs=tuple([pl.BlockSpec((B, ts, W), lambda t: (0, t, 0))] + c_out),
        scratch_shapes=[pltpu.VMEM((tm, S2), F32), pltpu.VMEM((tm, S2), F32),
                        pltpu.VMEM((4, B, S2 // 2), F32),
                        pltpu.VMEM((tm, tm), BF16), pltpu.VMEM((tm, tm), BF16)],
        compiler_params=_cparams(1),
        name="s5_mixer",
    )(proj3, bd0, bd1, ar, ai, c0, c1, dd, gw, gb, *c_args)


def _hgrn_kernel(xq_ref, xf_ref, xi_ref, xg_ref, lb_ref, ng_ref, o_ref,
                 st_ref, qe_ref, ke_ref, eb_ref, q_ref, k_ref, f_ref, b_ref, oi_ref):
    tl, W = xq_ref.shape
    C = HG_BLOCK
    hd = W // HG_HEADS
    nblk = tl // C

    Cl = HG_LAG_BLOCK
    level_sizes = []
    while (2 * Cl) << len(level_sizes) <= C:
        level_sizes.append((2 * Cl) << len(level_sizes))

    @pl.when(pl.program_id(1) == 0)
    def _():
        st_ref[...] = jnp.zeros_like(st_ref)

    row = lax.broadcasted_iota(jnp.int32, (tl, tl), 0)
    col = lax.broadcasted_iota(jnp.int32, (tl, tl), 1)
    same = (row // C) == (col // C)
    t_cum = jnp.where(same & (col <= row), 1.0, 0.0).astype(BF16)
    t_blk = jnp.where(same, 1.0, 0.0).astype(BF16)

    xq = xq_ref[...]
    q = xq * _sigmoid(xq)
    lb = lb_ref[...]
    f = lb + (1.0 - lb) * _sigmoid(xf_ref[...])
    kk = 1.0 - f
    g2 = jnp.log(f) * LOG2E

    g_hi = g2.astype(BF16)
    g_lo = (g2 - g_hi.astype(F32)).astype(BF16)
    b = _dot(t_cum, g_hi) + _dot(t_cum, g_lo)
    bl = _dot(t_blk, g_hi) + _dot(t_blk, g_lo)

    q_ref[...] = q
    k_ref[...] = kk
    f_ref[...] = f
    b_ref[...] = b
    qe_ref[...] = (q * jnp.exp2(b)).astype(BF16)
    ke_ref[...] = (kk * jnp.exp2(bl - b)).astype(BF16)
    eb_ref[...] = jnp.exp2(bl)

    heads = [slice(h * hd, (h + 1) * hd) for h in range(HG_HEADS)]
    st = [st_ref[h] for h in range(HG_HEADS)]
    for i in range(nblk):
        rows = slice(i * C, (i + 1) * C)
        for h, hl in enumerate(heads):
            oi_ref[rows, hl] = _dot_nt(qe_ref[rows, hl], st[h].astype(BF16))
            vb = xi_ref[rows, hl].astype(BF16)
            st[h] = st[h] * eb_ref[i * C:i * C + 1, hl] + _dot_tn(vb, ke_ref[rows, hl])
    for h in range(HG_HEADS):
        st_ref[h] = st[h]

    levels = []
    for size in level_sizes:
        half = size // 2
        anchor = jnp.concatenate(
            [jnp.broadcast_to(b_ref[j * size + half - 1:j * size + half, :], (size, W))
             for j in range(tl // size)], axis=0)
        eh = jnp.exp2(-jnp.abs(b_ref[...] - anchor))
        pair_ok = ((row // size) == (col // size)) & ((row % size) >= half) & ((col % size) < half)
        levels.append(((q_ref[...] * eh).astype(BF16), (k_ref[...] * eh).astype(BF16), pair_ok))
    lvl = []
    for hl in heads:
        w = jnp.zeros((tl, tl), F32)
        for qh, kh, pair_ok in levels:
            w = jnp.where(pair_ok, _dot_nt(qh[:, hl], kh[:, hl]), w)
        lvl.append(w)

    rc = HG_ROW_CHUNK if tl % HG_ROW_CHUNK == 0 else tl
    r_i = lax.broadcasted_iota(jnp.int32, (rc, tl), 0)
    c_i = lax.broadcasted_iota(jnp.int32, (rc, tl), 1)
    for r0 in range(0, tl, rc):
        lag = jnp.where(((r_i + r0) // Cl) == (c_i // Cl), r_i + r0 - c_i, -1)
        rows = slice(r0, r0 + rc)
        p = [k_ref[rows, hl] for hl in heads]
        a_mat = [jnp.zeros((rc, tl), F32) for _ in heads]
        for d in range(Cl):
            for h, hl in enumerate(heads):
                if d > 0:
                    p[h] = f_ref[rows, hl] * pltpu.roll(p[h], 1, axis=0)
                a = jnp.sum(q_ref[rows, hl] * p[h], axis=-1, keepdims=True)
                a_mat[h] = jnp.where(lag == d, a, a_mat[h])
        for h, hl in enumerate(heads):
            w = (a_mat[h] + lvl[h][r0:r0 + rc, :]).astype(BF16)
            acc = oi_ref[rows, hl] + _dot(w, xi_ref[:, hl].astype(BF16))
            on = acc * lax.rsqrt(jnp.mean(acc * acc, axis=-1, keepdims=True) + NORM_EPS)
            xg = xg_ref[rows, hl]
            o_ref[rows, hl] = (on * ng_ref[:, hl] * (xg * _sigmoid(xg))).astype(o_ref.dtype)


def _hgrn(proj, lb, ng, B, L, col0, tl=HG_ROW_TILE):
    N = proj.shape[0]
    W = lb.shape[1]
    hd = W // HG_HEADS
    tps = L // tl
    col = lambda c: pl.BlockSpec((tl, W), lambda b, t: (b * tps + t, col0 + c))
    return pl.pallas_call(
        _hgrn_kernel,
        out_shape=jax.ShapeDtypeStruct((N, W), BF16),
        grid=(B, tps),
        in_specs=[col(0), col(1), col(2), col(3), _resident((1, W)), _resident((1, W))],
        out_specs=pl.BlockSpec((tl, W), lambda b, t: (b * tps + t, 0)),
        scratch_shapes=[pltpu.VMEM((HG_HEADS, hd, hd), F32),
                        pltpu.VMEM((tl, W), BF16),
                        pltpu.VMEM((tl, W), BF16),
                        pltpu.VMEM((tl, W), F32),
                        pltpu.VMEM((tl, W), F32),
                        pltpu.VMEM((tl, W), F32),
                        pltpu.VMEM((tl, W), F32),
                        pltpu.VMEM((tl, W), F32),
                        pltpu.VMEM((tl, W), F32)],
        compiler_params=_cparams(2),
        name="hgrn2_mixer",
    )(proj, proj, proj, proj, lb, ng)


def _ffn_kernel(*refs, n_mix, tiles_per_seq, final_norm, tf):
    x_ref, xhalo_ref = refs[:2]
    mix_refs = refs[2:2 + n_mix]
    mixhalo_refs = refs[2 + n_mix:2 + 2 * n_mix]
    (wp_ref, g_ref, win_ref, cw_ref, cb_ref, wo_ref, gf_ref, o_ref,
     xn_ref, act_ref, a_ref, b_ref, h_ref) = refs[2 + 2 * n_mix:]
    i = pl.program_id(0)
    tm = x_ref.shape[0]
    H = FFN_HALO
    dff = wo_ref.shape[0]

    def mixed(r_ref, m_refs, rows):
        acc = r_ref[rows, :]
        k0 = 0
        for m_ref in m_refs:
            kw = m_ref.shape[1]
            acc = acc + _dot(m_ref[rows, :].astype(BF16), wp_ref[k0:k0 + kw, :])
            k0 += kw
        return acc

    hm = FFN_PART_ROWS
    parts = [(lo, lo + hm) for lo in range(0, tm, hm)]
    hn = _rms(mixed(xhalo_ref, mixhalo_refs, slice(0, H)), g_ref[...])
    xn_ref[0:H, :] = jnp.where(i % tiles_per_seq == 0, 0.0, hn).astype(BF16)
    for lo, hi in parts:
        h_ref[lo:hi, :] = mixed(x_ref, mix_refs, slice(lo, hi))
        xn_ref[H + lo:H + hi, :] = _rms(h_ref[lo:hi, :], g_ref[...]).astype(BF16)

    def conv(s_ref, c0, lo, hi):
        cols = slice(c0, c0 + tf)
        out = cb_ref[:, cols] + s_ref[H + lo:H + hi, :] * cw_ref[CONV_W - 1:CONV_W, cols]
        for t in range(1, CONV_W):
            out = out + s_ref[H + lo - t:H + hi - t, :] * cw_ref[CONV_W - 1 - t:CONV_W - t, cols]
        return out

    def up(j):
        for lo, hi in parts:
            strip = slice(0 if lo == 0 else H + lo, H + hi)
            a_ref[j % 2, strip, :] = _dot(xn_ref[strip, :], win_ref[:, j * tf:(j + 1) * tf])
            b_ref[j % 2, strip, :] = _dot(xn_ref[strip, :], win_ref[:, dff + j * tf:dff + (j + 1) * tf])

    def activate(j):
        for lo, hi in parts:
            a = conv(a_ref.at[j % 2], j * tf, lo, hi)
            b = conv(b_ref.at[j % 2], dff + j * tf, lo, hi)
            act_ref[lo:hi, j * tf:(j + 1) * tf] = (a * _sigmoid(a) * b).astype(BF16)

    nj = dff // tf
    for step in range(nj + 1):
        if step < nj:
            up(step)
        if step >= 1:
            activate(step - 1)

    for lo, hi in parts:
        y = h_ref[lo:hi, :] + _dot(act_ref[lo:hi, :], wo_ref[...])
        if final_norm:
            y = _rms(y, gf_ref[...])
        o_ref[lo:hi, :] = y


def _mix_ffn(x, mix_list, w_proj_bf, layer, g, w_in_bf, conv_w, conv_b, w_out_bf, g_final, L, final_norm,
             tm=FFN_ROW_TILE, tf=FFN_COL_TILE):
    N, D = x.shape
    dff = w_out_bf.shape[0]
    H = FFN_HALO
    main = lambda w: pl.BlockSpec((tm, w), lambda i: (i, 0))
    halo = lambda w: pl.BlockSpec((H, w), lambda i: (jnp.maximum(i * (tm // H) - 1, 0), 0))
    kern = functools.partial(_ffn_kernel, n_mix=len(mix_list), tiles_per_seq=L // tm,
                             final_norm=final_norm, tf=tf)
    return pl.pallas_call(
        kern,
        out_shape=jax.ShapeDtypeStruct((N, D), F32),
        grid=(N // tm,),
        in_specs=[main(D), halo(D)]
                 + [main(m.shape[1]) for m in mix_list] + [halo(m.shape[1]) for m in mix_list]
                 + [_resident(w_proj_bf.shape), _resident_layer(g.shape, layer), _resident(w_in_bf.shape),
                    _resident_layer(conv_w.shape, layer), _resident_layer(conv_b.shape, layer),
                    _resident(w_out_bf.shape), _resident((1, D))],
        out_specs=pl.BlockSpec((tm, D), lambda i: (i, 0)),
        scratch_shapes=[pltpu.VMEM((tm + H, D), BF16),
                        pltpu.VMEM((tm, dff), BF16),
                        pltpu.VMEM((2, tm + H, tf), F32),
                        pltpu.VMEM((2, tm + H, tf), F32),
                        pltpu.VMEM((tm, D), F32)],
        compiler_params=_cparams(1),
        name="mix_conv_ffn",
    )(x, x, *mix_list, *mix_list, w_proj_bf, g, w_in_bf, conv_w, conv_b, w_out_bf, g_final)


def _rot_lane_layout():
    half = ROT_DIM // 2
    E = ATT_HEAD_DIM
    lay = []
    for base in (0, half):
        lay += [(0, base + i) for i in range(half)] + [(1, base + i) for i in range(half)]
        lay += [(base // half, d) for d in range(ROT_DIM, E)]
    return lay


def _attn_kernel(q_ref, k_ref, v_ref, o_ref, m_a_ref, m_b_ref, acc_a_ref, acc_b_ref, tmp_ref,
                 qs_ref, ka_ref, kb_ref, va_ref, vb_ref):
    grp = pl.program_id(2)
    L, lw = q_ref.shape
    T = ATT_BLOCK
    E = ATT_HEAD_DIM
    n_groups = len(ATT_DILATIONS)

    lane1 = lax.broadcasted_iota(jnp.int32, (1, lw), 1)
    head_of_lane = np.array([h for h, _ in _rot_lane_layout()])
    edges = np.flatnonzero(np.diff(np.concatenate([[1], head_of_lane, [1]])))
    in_a_qk = jnp.zeros((1, lw), jnp.bool_)
    for lo, hi in zip(edges[0::2], edges[1::2]):
        in_a_qk = in_a_qk | ((lane1 >= int(lo)) & (lane1 < int(hi)))
    in_a_v = lane1 < E
    qi = lax.broadcasted_iota(jnp.int32, (T, T), 0)
    kj = lax.broadcasted_iota(jnp.int32, (T, T), 1)

    S = ATT_DILATIONS[1]
    assert ATT_DILATIONS == (1, S, S * S) and T % S == 0
    Ls = L // S

    def by_s_rows(dil, r, n):
        if dil == S:
            return slice(r * Ls + n * T, r * Ls + (n + 1) * T)
        return pl.ds((r % S) * Ls + r // S, T, stride=S)

    def run_group(gi, dil):
        M = L // dil
        nblk = M // T

        def residue_major(x_ref, emit):
            if dil == S * S:
                for r4 in range(S):
                    tmp_ref[r4 * Ls:(r4 + 1) * Ls, :] = x_ref[pl.ds(r4, Ls, stride=S), :]
            for r in range(dil):
                if dil == 1:
                    xr = x_ref[...]
                elif dil == S:
                    xr = x_ref[pl.ds(r, M, stride=S), :]
                else:
                    xr = tmp_ref[by_s_rows(dil, r, 0), :]
                emit(slice(r * M, (r + 1) * M), xr)

        def emit_q(dst, xr):
            qs_ref[dst, :] = (xr * (E ** -0.5 * LOG2E)).astype(BF16)

        def emit_k(dst, xr):
            ka_ref[dst, :] = jnp.where(in_a_qk, xr, 0.0).astype(BF16)
            kb_ref[dst, :] = jnp.where(in_a_qk, 0.0, xr).astype(BF16)

        def emit_v(dst, xr):
            va_ref[dst, :] = jnp.where(in_a_v, xr, 1.0).astype(BF16)
            vb_ref[dst, :] = jnp.where(in_a_v, 1.0, xr).astype(BF16)

        if dil == 1:
            for n in range(nblk):
                for j in range(S):
                    emit_q(slice(n * T + j * (T // S), n * T + (j + 1) * (T // S)),
                           q_ref[pl.ds(n * T + j, T // S, stride=S), :])
            q_pos = S * (qi % (T // S)) + qi // (T // S)
        else:
            residue_major(q_ref, emit_q)
            q_pos = qi
        cur_ok = kj <= q_pos
        both_ok = jnp.concatenate([kj >= q_pos, cur_ok], axis=1)
        residue_major(k_ref, emit_k)
        residue_major(v_ref, emit_v)

        k_refs, v_refs = (ka_ref, kb_ref), (va_ref, vb_ref)
        m_refs, acc_refs = (m_a_ref, m_b_ref), (acc_a_ref, acc_b_ref)
        blocks = [(r, n) for r in range(dil) for n in range(nblk)]

        def load_state(ref, r, n):
            return ref[by_s_rows(dil, r, n), :]

        def store_state(ref, r, n, val):
            if dil == 1:
                for j in range(S):
                    ref[j * Ls + (T // S) * n:j * Ls + (T // S) * (n + 1), :] = (
                        val[j * (T // S):(j + 1) * (T // S), :])
            else:
                ref[by_s_rows(dil, r, n), :] = val

        def kv_rows(r, n):
            c0 = r * M + n * T
            return slice(c0 - T, c0 + T) if n > 0 else slice(c0, c0 + T)

        def scores(r, n, hh):
            c0 = r * M + n * T
            return _dot_nt(qs_ref[c0:c0 + T, :], k_refs[hh][kv_rows(r, n), :])

        def softmax(r, n, hh, s):
            s = jnp.where(both_ok if n > 0 else cur_ok, s, NEG_BIG)
            m = jnp.broadcast_to(jnp.max(s, axis=-1, keepdims=True), (T, lw))
            alpha = None
            if gi > 0:
                m_old = load_state(m_refs[hh], r, n)
                m = jnp.maximum(m, m_old)
                alpha = jnp.exp2(m_old - m)
            if gi < n_groups - 1:
                store_state(m_refs[hh], r, n, m)
            if n > 0:
                m = jnp.concatenate([m, m], axis=1)
            return jnp.exp2(s - m).astype(BF16), alpha

        pv_first = {}

        def values(r, n, hh, p, alpha):
            pv = _dot(p, v_refs[hh][kv_rows(r, n), :])
            if gi > 0:
                pv = alpha * load_state(acc_refs[hh], r, n) + pv
            if gi < n_groups - 1:
                store_state(acc_refs[hh], r, n, pv)
            elif hh == 0:
                pv_first[(r, n)] = pv / pltpu.roll(pv, E, axis=1)
            else:
                tmp_ref[by_s_rows(dil, r, n), :] = jnp.where(
                    in_a_v, pv_first.pop((r, n)), pv / pltpu.roll(pv, E, axis=1))

        s_val, p_val = {}, {}
        for step in range(len(blocks) + 3):
            if step < len(blocks):
                s_val[step] = [scores(*blocks[step], hh) for hh in range(2)]
            if 0 <= step - 1 < len(blocks):
                s_pair = s_val.pop(step - 1)
                p_val[step - 1] = [softmax(*blocks[step - 1], hh, s_pair[hh]) for hh in range(2)]
            if 0 <= step - 3 < len(blocks):
                p_pair = p_val.pop(step - 3)
                for hh in range(2):
                    values(*blocks[step - 3], hh, *p_pair[hh])

        if gi == n_groups - 1:
            for r4 in range(S):
                o_ref[pl.ds(r4, Ls, stride=S), :] = tmp_ref[r4 * Ls:(r4 + 1) * Ls, :]

    for gi, dil in enumerate(ATT_DILATIONS):
        pl.when(grp == gi)(functools.partial(run_group, gi, dil))


def _attn(qkv, B, L, gw):
    N = qkv.shape[0]
    lw = 2 * ATT_HEAD_DIM
    npair = gw // lw
    ng = len(ATT_DILATIONS)

    def spec(kind):
        return pl.BlockSpec((L, lw), lambda b, p, g: (b, kind * ng * npair + g * npair + p))

    return pl.pallas_call(
        _attn_kernel,
        out_shape=jax.ShapeDtypeStruct((N, gw), F32),
        grid=(B, npair, ng),
        in_specs=[spec(0), spec(1), spec(2)],
        out_specs=pl.BlockSpec((L, lw), lambda b, p, g: (b, p)),
        scratch_shapes=[pltpu.VMEM((L, lw), F32)] * 4
                       + [pltpu.VMEM((L, lw), F32)]
                       + [pltpu.VMEM((L, lw), BF16)] * 5,
        compiler_params=_cparams(3),
        name="dilated_attention",
    )(qkv, qkv, qkv)


def _s5_params(A_re, A_im, log_dt, B_re, B_im, C_re, C_im, Dd):
    G, P = A_re.shape
    dt = jnp.exp(log_dt)[:, None]
    mag = jnp.exp(A_re * dt)
    ab_re = mag * jnp.cos(A_im * dt)
    ab_im = mag * jnp.sin(A_im * dt)
    den = A_re * A_re + A_im * A_im
    nr, ni = ab_re - 1.0, ab_im
    c_re = (nr * A_re + ni * A_im) / den
    c_im = (ni * A_re - nr * A_im) / den
    Bb_re = c_re[..., None] * B_re - c_im[..., None] * B_im
    Bb_im = c_re[..., None] * B_im + c_im[..., None] * B_re
    hg = G // 2
    eye = jnp.eye(hg, dtype=F32)

    def bd_in(m):
        return jnp.einsum('gpc,gh->gchp', m, eye).reshape(hg * S5_GROUP, hg * P)

    def bd_out(m):
        return jnp.einsum('gcp,gh->gphc', m, eye).reshape(hg * P, hg * S5_GROUP)

    bds, cds = [], []
    for k in range(2):
        sl = slice(k * hg, (k + 1) * hg)
        bds.append(jnp.concatenate([bd_in(Bb_re[sl]), bd_in(Bb_im[sl])], axis=1).astype(BF16))
        cds.append(jnp.concatenate([bd_out(C_re[sl]), -bd_out(C_im[sl])], axis=0).astype(BF16))
    return bds[0], bds[1], ab_re.reshape(1, G * P), ab_im.reshape(1, G * P), cds[0], cds[1], Dd.reshape(1, -1)


ROPE_COLS = 32


def _rope_tables(positions):
    half = ROT_DIM // 2
    inv_freq = ROPE_THETA ** (-jnp.arange(half, dtype=F32) * 2.0 / ROT_DIM)
    ang = positions.astype(F32).reshape(-1, 1) * inv_freq
    n = ang.shape[0]
    cs = jnp.concatenate([jnp.cos(ang), jnp.sin(ang), jnp.ones((n, 1), F32),
                          jnp.zeros((n, ROPE_COLS - 2 * half - 1), F32)], axis=1)
    ex = np.zeros((ROPE_COLS, 2 * LANES), np.float32)
    for lane, (_, d) in enumerate(_rot_lane_layout()):
        if d < half:
            ex[d, lane] = 1.0
            ex[half + d, LANES + lane] = -1.0
        elif d < ROT_DIM:
            ex[d - half, lane] = 1.0
            ex[half + d - half, LANES + lane] = 1.0
        else:
            ex[2 * half, lane] = 1.0
    return cs, jnp.asarray(ex, dtype=BF16)


def _qk_lane_layout(w):
    rows, n_cols = w.shape
    E = ATT_HEAD_DIM
    wp = w.reshape(rows, n_cols // LANES, 2, E)
    lay = _rot_lane_layout()
    pieces, start = [], 0
    for l in range(1, LANES + 1):
        if l == LANES or lay[l][0] != lay[start][0] or lay[l][1] != lay[l - 1][1] + 1:
            h, d0 = lay[start]
            pieces.append(wp[:, :, h, d0:d0 + (l - start)])
            start = l
    return jnp.concatenate(pieces, axis=-1).reshape(rows, n_cols)


def kernel(x, positions, norm_mix, norm_ffn, norm_final, mix_w_in, mix_w_out, s5_A_re, s5_A_im, s5_log_dt, s5_B_re, s5_B_im, s5_C_re, s5_C_im, s5_D, s5_glu_w, s5_glu_b, hgrn_gamma, hgrn_norm, att_w_qkv, att_w_o, ffn_w_in, ffn_conv_w, ffn_conv_b, ffn_w_out):
    B, L, D = x.shape
    N = B * L
    x2 = x.reshape(N, D)
    tm = PROJ_ROW_TILE
    s5w = s5_A_re.shape[1] * S5_GROUP
    hgw = hgrn_norm.shape[1]

    ffn_cast = lambda layer, steps: ((ffn_w_in, layer, D // steps), (ffn_w_out, layer, FFN_COL_TILE))
    g_ffn, conv_b = norm_ffn[:, None, :], ffn_conv_b[:, None, :]
    g_fin = norm_final.reshape(1, D)

    proj = _norm_proj(x2, norm_mix[0:1], mix_w_in[0].astype(BF16), tm, s5w, name="inproj")[0]
    s5p = _s5_params(s5_A_re[0], s5_A_im[0], s5_log_dt[0], s5_B_re[0], s5_B_im[0],
                     s5_C_re[0], s5_C_im[0], s5_D[0])
    oa, w_in0, w_out0 = _s5(proj.reshape(B, L, -1), *s5p, s5_glu_w[0].astype(BF16), s5_glu_b[0:1], s5w,
                            cast=ffn_cast(0, L // S5_TIME_TILE))
    lb_all = jnp.cumsum(jax.nn.softmax(hgrn_gamma.astype(F32), axis=0), axis=0)
    ob = _hgrn(proj, lb_all[0:1], hgrn_norm[0:1], B, L, col0=s5w // hgw)
    h = _mix_ffn(x2, [oa.reshape(N, s5w), ob], mix_w_out[0].astype(BF16), 0, g_ffn, w_in0, ffn_conv_w,
                 conv_b, w_out0, g_fin, L, final_norm=False)

    gw = att_w_o.shape[1]
    n_rot_cols = 2 * len(ATT_DILATIONS) * gw
    w_qkv = att_w_qkv[0].astype(BF16)
    w_qkv = jnp.concatenate([_qk_lane_layout(w_qkv[:, :n_rot_cols]), w_qkv[:, n_rot_cols:]], axis=1)
    qkv, w_in1, w_out1 = _norm_proj(h, norm_mix[1:2], w_qkv, tm, gw, rot=_rope_tables(positions),
                                    n_rot_cols=n_rot_cols, cast=ffn_cast(1, N // tm), name="qkv_rotary")
    ao = _attn(qkv, B, L, gw)
    h = _mix_ffn(h, [ao], att_w_o[0].astype(BF16), 1, g_ffn, w_in1, ffn_conv_w, conv_b, w_out1, g_fin, L,
                 final_norm=True)
    return h.reshape(B, L, D)
```

```python
import functools
import math

import numpy as np
import jax
import jax.numpy as jnp
from jax import lax
from jax.experimental import pallas as pl
from jax.experimental.pallas import tpu as pltpu

F32 = jnp.float32
BF16 = jnp.bfloat16

NORM_EPS = 1e-6
S5_GROUP = 16
S5_STATE = 64
HG_HEADS = 4
HG_BLOCK = 128
HG_LAG_BLOCK = 4
ATT_HEAD_DIM = 64
ATT_BLOCK = 128
ATT_DILATIONS = (1, 4, 16)
ROT_DIM = 16
ROPE_THETA = 500000.0
CONV_W = 3
NEG_BIG = -1e30
LOG2E = math.log2(math.e)
LANES = 128

PROJ_ROW_TILE = 512
FFN_ROW_TILE = 1024
FFN_COL_TILE = 256
FFN_PART_ROWS = 256
FFN_HALO = 16
S5_TIME_TILE = 64
S5_LANE_CHUNK = 1024
HG_ROW_TILE = 128
HG_ROW_CHUNK = 64

VMEM_PHYSICAL = 64 * 1024 * 1024
VMEM_LIMIT = VMEM_PHYSICAL - 4 * 1024 * 1024


def _cparams(n_axes):
    return pltpu.CompilerParams(dimension_semantics=("arbitrary",) * n_axes,
                                vmem_limit_bytes=VMEM_LIMIT)


def _resident(shape):
    return pl.BlockSpec(shape, lambda *_: (0,) * len(shape), pipeline_mode=pl.Buffered(1))


def _resident_layer(shape, layer):
    return pl.BlockSpec((None,) + tuple(shape[1:]), lambda *_: (layer,) + (0,) * (len(shape) - 1),
                        pipeline_mode=pl.Buffered(1))


def _rms(x, g):
    return x * lax.rsqrt(jnp.mean(x * x, axis=-1, keepdims=True) + NORM_EPS) * g


def _sigmoid(x):
    return 1.0 / (1.0 + jnp.exp(-x))


def _dot(a, b):
    return jnp.dot(a, b, preferred_element_type=F32)


def _dot_nt(a, b):
    return lax.dot_general(a, b, (((1,), (1,)), ((), ())), preferred_element_type=F32)


def _dot_tn(a, b):
    return lax.dot_general(a, b, (((0,), (0,)), ((), ())), preferred_element_type=F32)


def _norm_proj_kernel(x_ref, g_ref, w_ref, *rest, tn, n_rot_tiles, n_cast):
    rest = list(rest)
    xn_ref = rest.pop()
    cast_out = [rest.pop() for _ in range(n_cast)][::-1]
    o_ref = rest.pop()
    cast_in = [rest.pop() for _ in range(n_cast)][::-1]
    for src, dst in zip(cast_in, cast_out):
        dst[...] = src[...].astype(BF16)
    if n_rot_tiles:
        cs_ref, ex_ref = rest
        cs = cs_ref[...]
        cs_hi = cs.astype(BF16)
        cs_lo = (cs - cs_hi.astype(F32)).astype(BF16)
        tab = _dot(cs_hi, ex_ref[...]) + _dot(cs_lo, ex_ref[...])
        c = tab[:, :LANES]
        s = tab[:, LANES:]
    tm = x_ref.shape[0]
    parts = [slice(0, tm // 2), slice(tm // 2, tm)]
    for rows in parts:
        xn_ref[rows, :] = _rms(x_ref[rows, :], g_ref[...]).astype(BF16)
    for j in range(w_ref.shape[1] // tn):
        for rows in parts:
            y = _dot(xn_ref[rows, :], w_ref[:, j * tn:(j + 1) * tn])
            if j < n_rot_tiles:
                for k in range(tn // LANES):
                    t = y[:, k * LANES:(k + 1) * LANES]
                    o_ref[rows, j * tn + k * LANES:j * tn + (k + 1) * LANES] = (
                        t * c[rows, :] + pltpu.roll(t, LANES // 2, axis=1) * s[rows, :])
            else:
                o_ref[rows, j * tn:(j + 1) * tn] = y.astype(o_ref.dtype)


def _cast_side_job(cast, steps):
    in_specs, args, out_shape, out_specs = [], [], [], []
    for arr, layer, rows in cast:
        _, r_total, cols = arr.shape
        last = r_total // rows - 1
        assert r_total % rows == 0 and last < steps
        in_specs.append(pl.BlockSpec((None, rows, cols),
                                     lambda i, layer=layer, last=last: (layer, jnp.minimum(i, last), 0)))
        args.append(arr)
        out_shape.append(jax.ShapeDtypeStruct((r_total, cols), BF16))
        out_specs.append(pl.BlockSpec((rows, cols), lambda i, last=last: (jnp.minimum(i, last), 0)))
    return in_specs, args, out_shape, out_specs


def _norm_proj(x2, g, w_bf, tm, tn, rot=None, n_rot_cols=0, cast=(), out_dtype=F32, name="norm_proj"):
    N, D = x2.shape
    ncols = w_bf.shape[1]
    steps = N // tm
    in_specs = [pl.BlockSpec((tm, D), lambda i: (i, 0)), _resident((1, D)), _resident(w_bf.shape)]
    args = [x2, g, w_bf]
    if rot is not None:
        cs, expand = rot
        in_specs += [pl.BlockSpec((tm, cs.shape[1]), lambda i: (i, 0)), _resident(expand.shape)]
        args += [cs, expand]
    out_shape = [jax.ShapeDtypeStruct((N, ncols), out_dtype)]
    out_specs = [pl.BlockSpec((tm, ncols), lambda i: (i, 0))]
    c_in, c_args, c_shape, c_out = _cast_side_job(cast, steps)
    in_specs += c_in
    args += c_args
    out_shape += c_shape
    out_specs += c_out
    return pl.pallas_call(
        functools.partial(_norm_proj_kernel, tn=tn, n_rot_tiles=n_rot_cols // tn, n_cast=len(cast)),
        out_shape=tuple(out_shape),
        grid=(steps,),
        in_specs=in_specs,
        out_specs=tuple(out_specs),
        scratch_shapes=[pltpu.VMEM((tm, D), BF16)],
        compiler_params=_cparams(1),
        name=name,
    )(*args)


def _s5_kernel(u_ref, bd0_ref, bd1_ref, ar_ref, ai_ref, c0_ref, c1_ref, d_ref, gw_ref, gb_ref,
               *rest, lane_chunk, n_cast):
    cast_in = rest[:n_cast]
    o_ref = rest[n_cast]
    cast_out = rest[n_cast + 1:2 * n_cast + 1]
    x0_ref, x1_ref, st_ref, pf_ref, pb_ref = rest[2 * n_cast + 1:]
    for src, dst in zip(cast_in, cast_out):
        dst[...] = src[...].astype(BF16)
    nb, ts, W = u_ref.shape
    tm = nb * ts
    half = x0_ref.shape[1] // 2

    @pl.when(pl.program_id(0) == 0)
    def _():
        st_ref[...] = jnp.zeros_like(st_ref)
        r = lax.broadcasted_iota(jnp.int32, (tm, tm), 0)
        c = lax.broadcasted_iota(jnp.int32, (tm, tm), 1)
        pf_ref[...] = jnp.where(c == (r % nb) * ts + r // nb, 1.0, 0.0).astype(BF16)
        pb_ref[...] = jnp.where(r == (c % nb) * ts + c // nb, 1.0, 0.0).astype(BF16)

    ut_hi = _dot(pf_ref[...], u_ref[...].reshape(tm, W))
    u = ut_hi
    ub = ut_hi.astype(BF16)
    ku = W // 2
    x0_ref[...] = _dot(ub[:, :ku], bd0_ref[...])
    x1_ref[...] = _dot(ub[:, ku:], bd1_ref[...])

    def scan(k, x_ref):
        for c in range(half // lane_chunk):
            lo = c * lane_chunk
            re = slice(lo, lo + lane_chunk)
            im = slice(half + lo, half + lo + lane_chunk)
            a_re = jnp.broadcast_to(ar_ref[:, k * half + lo:k * half + lo + lane_chunk], (nb, lane_chunk))
            a_im = jnp.broadcast_to(ai_ref[:, k * half + lo:k * half + lo + lane_chunk], (nb, lane_chunk))
            xr, xi = st_ref[2 * k, :, re], st_ref[2 * k + 1, :, re]
            for s in range(ts):
                rows = slice(s * nb, (s + 1) * nb)
                xr, xi = (a_re * xr - a_im * xi + x_ref[rows, re],
                          a_re * xi + a_im * xr + x_ref[rows, im])
                x_ref[rows, re] = xr
                x_ref[rows, im] = xi
            st_ref[2 * k, :, re] = xr
            st_ref[2 * k + 1, :, re] = xi

    scan(0, x0_ref)
    y0 = _dot(x0_ref[...].astype(BF16), c0_ref[...])
    scan(1, x1_ref)
    y1 = _dot(x1_ref[...].astype(BF16), c1_ref[...])
    y = jnp.concatenate([y0, y1], axis=1) + d_ref[...] * u
    z = 0.5 * y * (1.0 + jnp.tanh(math.sqrt(2.0 / math.pi) * (y + 0.044715 * (y * y * y))))
    gate = _sigmoid(_dot(z.astype(BF16), gw_ref[...]) + gb_ref[...])
    o_tm = (z * gate).astype(BF16)
    o_ref[...] = _dot(pb_ref[...], o_tm).astype(BF16).reshape(nb, ts, W)


def _s5(proj3, bd0, bd1, ar, ai, c0, c1, dd, gw, gb, W, cast=(), ts=S5_TIME_TILE):
    B, L, _ = proj3.shape
    S2 = bd0.shape[1]
    tm = B * ts
    c_in, c_args, c_shape, c_out = _cast_side_job(cast, L // ts)
    kern = functools.partial(_s5_kernel, lane_chunk=S5_LANE_CHUNK, n_cast=len(cast))
    return pl.pallas_call(
        kern,
        out_shape=tuple([jax.ShapeDtypeStruct((B, L, W), BF16)] + c_shape),
        grid=(L // ts,),
        in_specs=[pl.BlockSpec((B, ts, W), lambda t: (0, t, 0))]
                 + [_resident(a.shape) for a in (bd0, bd1, ar, ai, c0, c1, dd, gw, gb)] + c_in,
        out_specs=tuple([pl.BlockSpec((B, ts, W), lambda t: (0, t, 0))] + c_out),
        scratch_shapes=[pltpu.VMEM((tm, S2), F32), pltpu.VMEM((tm, S2), F32),
                        pltpu.VMEM((4, B, S2 // 2), F32),
                        pltpu.VMEM((tm, tm), BF16), pltpu.VMEM((tm, tm), BF16)],
        compiler_params=_cparams(1),
        name="s5_mixer",
    )(proj3, bd0, bd1, ar, ai, c0, c1, dd, gw, gb, *c_args)


def _hgrn_kernel(xq_ref, xf_ref, xi_ref, xg_ref, lb_ref, ng_ref, o_ref,
                 st_ref, qe_ref, ke_ref, eb_ref, q_ref, k_ref, f_ref, b_ref, oi_ref):
    tl, W = xq_ref.shape
    C = HG_BLOCK
    hd = W // HG_HEADS
    nblk = tl // C

    Cl = HG_LAG_BLOCK
    level_sizes = []
    while (2 * Cl) << len(level_sizes) <= C:
        level_sizes.append((2 * Cl) << len(level_sizes))

    @pl.when(pl.program_id(1) == 0)
    def _():
        st_ref[...] = jnp.zeros_like(st_ref)

    row = lax.broadcasted_iota(jnp.int32, (tl, tl), 0)
    col = lax.broadcasted_iota(jnp.int32, (tl, tl), 1)
    same = (row // C) == (col // C)
    t_cum = jnp.where(same & (col <= row), 1.0, 0.0).astype(BF16)
    t_blk = jnp.where(same, 1.0, 0.0).astype(BF16)

    xq = xq_ref[...].astype(F32)
    q = xq * _sigmoid(xq)
    lb = lb_ref[...]
    f = lb + (1.0 - lb) * _sigmoid(xf_ref[...].astype(F32))
    kk = 1.0 - f
    g2 = jnp.log(f) * LOG2E

    g_hi = g2.astype(BF16)
    g_lo = (g2 - g_hi.astype(F32)).astype(BF16)
    b = _dot(t_cum, g_hi) + _dot(t_cum, g_lo)
    bl = _dot(t_blk, g_hi) + _dot(t_blk, g_lo)

    q_ref[...] = q
    k_ref[...] = kk
    f_ref[...] = f
    b_ref[...] = b
    qe_ref[...] = (q * jnp.exp2(b)).astype(BF16)
    ke_ref[...] = (kk * jnp.exp2(bl - b)).astype(BF16)
    eb_ref[...] = jnp.exp2(bl)

    heads = [slice(h * hd, (h + 1) * hd) for h in range(HG_HEADS)]
    st = [st_ref[h] for h in range(HG_HEADS)]
    for i in range(nblk):
        rows = slice(i * C, (i + 1) * C)
        for h, hl in enumerate(heads):
            oi_ref[rows, hl] = _dot_nt(qe_ref[rows, hl], st[h].astype(BF16))
            vb = xi_ref[rows, hl].astype(BF16)
            st[h] = st[h] * eb_ref[i * C:i * C + 1, hl] + _dot_tn(vb, ke_ref[rows, hl])
    for h in range(HG_HEADS):
        st_ref[h] = st[h]

    levels = []
    for size in level_sizes:
        half = size // 2
        anchor = jnp.concatenate(
            [jnp.broadcast_to(b_ref[j * size + half - 1:j * size + half, :], (size, W))
             for j in range(tl // size)], axis=0)
        eh = jnp.exp2(-jnp.abs(b_ref[...] - anchor))
        pair_ok = ((row // size) == (col // size)) & ((row % size) >= half) & ((col % size) < half)
        levels.append(((q_ref[...] * eh).astype(BF16), (k_ref[...] * eh).astype(BF16), pair_ok))
    lvl = []
    for hl in heads:
        w = jnp.zeros((tl, tl), F32)
        for qh, kh, pair_ok in levels:
            w = jnp.where(pair_ok, _dot_nt(qh[:, hl], kh[:, hl]), w)
        lvl.append(w)

    rc = HG_ROW_CHUNK if tl % HG_ROW_CHUNK == 0 else tl
    r_i = lax.broadcasted_iota(jnp.int32, (rc, tl), 0)
    c_i = lax.broadcasted_iota(jnp.int32, (rc, tl), 1)
    for r0 in range(0, tl, rc):
        lag = jnp.where(((r_i + r0) // Cl) == (c_i // Cl), r_i + r0 - c_i, -1)
        rows = slice(r0, r0 + rc)
        p = [k_ref[rows, hl] for hl in heads]
        a_mat = [jnp.zeros((rc, tl), F32) for _ in heads]
        for d in range(Cl):
            for h, hl in enumerate(heads):
                if d > 0:
                    p[h] = f_ref[rows, hl] * pltpu.roll(p[h], 1, axis=0)
                a = jnp.sum(q_ref[rows, hl] * p[h], axis=-1, keepdims=True)
                a_mat[h] = jnp.where(lag == d, a, a_mat[h])
        for h, hl in enumerate(heads):
            w = (a_mat[h] + lvl[h][r0:r0 + rc, :]).astype(BF16)
            acc = oi_ref[rows, hl] + _dot(w, xi_ref[:, hl].astype(BF16))
            on = acc * lax.rsqrt(jnp.mean(acc * acc, axis=-1, keepdims=True) + NORM_EPS)
            xg = xg_ref[rows, hl].astype(F32)
            o_ref[rows, hl] = (on * ng_ref[:, hl] * (xg * _sigmoid(xg))).astype(o_ref.dtype)


def _hgrn(proj, lb, ng, B, L, col0, tl=HG_ROW_TILE):
    N = proj.shape[0]
    W = lb.shape[1]
    hd = W // HG_HEADS
    tps = L // tl
    col = lambda c: pl.BlockSpec((tl, W), lambda b, t: (b * tps + t, col0 + c))
    return pl.pallas_call(
        _hgrn_kernel,
        out_shape=jax.ShapeDtypeStruct((N, W), BF16),
        grid=(B, tps),
        in_specs=[col(0), col(1), col(2), col(3), _resident((1, W)), _resident((1, W))],
        out_specs=pl.BlockSpec((tl, W), lambda b, t: (b * tps + t, 0)),
        scratch_shapes=[pltpu.VMEM((HG_HEADS, hd, hd), F32),
                        pltpu.VMEM((tl, W), BF16),
                        pltpu.VMEM((tl, W), BF16),
                        pltpu.VMEM((tl, W), F32),
                        pltpu.VMEM((tl, W), F32),
                        pltpu.VMEM((tl, W), F32),
                        pltpu.VMEM((tl, W), F32),
                        pltpu.VMEM((tl, W), F32),
                        pltpu.VMEM((tl, W), F32)],
        compiler_params=_cparams(2),
        name="hgrn2_mixer",
    )(proj, proj, proj, proj, lb, ng)


def _ffn_kernel(*refs, n_mix, tiles_per_seq, final_norm, tf):
    x_ref, xhalo_ref = refs[:2]
    mix_refs = refs[2:2 + n_mix]
    mixhalo_refs = refs[2 + n_mix:2 + 2 * n_mix]
    (wp_ref, g_ref, win_ref, cw_ref, cb_ref, wo_ref, gf_ref, o_ref,
     xn_ref, act_ref, a_ref, b_ref, h_ref) = refs[2 + 2 * n_mix:]
    i = pl.program_id(0)
    tm = x_ref.shape[0]
    H = FFN_HALO
    dff = wo_ref.shape[0]

    def mixed(r_ref, m_refs, rows):
        acc = r_ref[rows, :]
        k0 = 0
        for m_ref in m_refs:
            kw = m_ref.shape[1]
            acc = acc + _dot(m_ref[rows, :].astype(BF16), wp_ref[k0:k0 + kw, :])
            k0 += kw
        return acc

    hm = FFN_PART_ROWS
    parts = [(lo, lo + hm) for lo in range(0, tm, hm)]
    hn = _rms(mixed(xhalo_ref, mixhalo_refs, slice(0, H)), g_ref[...])
    xn_ref[0:H, :] = jnp.where(i % tiles_per_seq == 0, 0.0, hn).astype(BF16)
    for lo, hi in parts:
        h_ref[lo:hi, :] = mixed(x_ref, mix_refs, slice(lo, hi))
        xn_ref[H + lo:H + hi, :] = _rms(h_ref[lo:hi, :], g_ref[...]).astype(BF16)

    def conv(s_ref, c0, lo, hi):
        cols = slice(c0, c0 + tf)
        out = cb_ref[:, cols] + s_ref[H + lo:H + hi, :] * cw_ref[CONV_W - 1:CONV_W, cols]
        for t in range(1, CONV_W):
            out = out + s_ref[H + lo - t:H + hi - t, :] * cw_ref[CONV_W - 1 - t:CONV_W - t, cols]
        return out

    def up(j):
        for lo, hi in parts:
            strip = slice(0 if lo == 0 else H + lo, H + hi)
            a_ref[j % 2, strip, :] = _dot(xn_ref[strip, :], win_ref[:, j * tf:(j + 1) * tf])
            b_ref[j % 2, strip, :] = _dot(xn_ref[strip, :], win_ref[:, dff + j * tf:dff + (j + 1) * tf])

    def activate(j):
        for lo, hi in parts:
            a = conv(a_ref.at[j % 2], j * tf, lo, hi)
            b = conv(b_ref.at[j % 2], dff + j * tf, lo, hi)
            act_ref[lo:hi, j * tf:(j + 1) * tf] = (a * _sigmoid(a) * b).astype(BF16)

    nj = dff // tf
    for step in range(nj + 1):
        if step < nj:
            up(step)
        if step >= 1:
            activate(step - 1)

    for lo, hi in parts:
        y = h_ref[lo:hi, :] + _dot(act_ref[lo:hi, :], wo_ref[...])
        if final_norm:
            y = _rms(y, gf_ref[...])
        o_ref[lo:hi, :] = y


def _mix_ffn(x, mix_list, w_proj_bf, layer, g, w_in_bf, conv_w, conv_b, w_out_bf, g_final, L, final_norm,
             tm=FFN_ROW_TILE, tf=FFN_COL_TILE):
    N, D = x.shape
    dff = w_out_bf.shape[0]
    H = FFN_HALO
    main = lambda w: pl.BlockSpec((tm, w), lambda i: (i, 0))
    halo = lambda w: pl.BlockSpec((H, w), lambda i: (jnp.maximum(i * (tm // H) - 1, 0), 0))
    kern = functools.partial(_ffn_kernel, n_mix=len(mix_list), tiles_per_seq=L // tm,
                             final_norm=final_norm, tf=tf)
    return pl.pallas_call(
        kern,
        out_shape=jax.ShapeDtypeStruct((N, D), F32),
        grid=(N // tm,),
        in_specs=[main(D), halo(D)]
                 + [main(m.shape[1]) for m in mix_list] + [halo(m.shape[1]) for m in mix_list]
                 + [_resident(w_proj_bf.shape), _resident_layer(g.shape, layer), _resident(w_in_bf.shape),
                    _resident_layer(conv_w.shape, layer), _resident_layer(conv_b.shape, layer),
                    _resident(w_out_bf.shape), _resident((1, D))],
        out_specs=pl.BlockSpec((tm, D), lambda i: (i, 0)),
        scratch_shapes=[pltpu.VMEM((tm + H, D), BF16),
                        pltpu.VMEM((tm, dff), BF16),
                        pltpu.VMEM((2, tm + H, tf), F32),
                        pltpu.VMEM((2, tm + H, tf), F32),
                        pltpu.VMEM((tm, D), F32)],
        compiler_params=_cparams(1),
        name="mix_conv_ffn",
    )(x, x, *mix_list, *mix_list, w_proj_bf, g, w_in_bf, conv_w, conv_b, w_out_bf, g_final)


def _rot_lane_layout():
    half = ROT_DIM // 2
    E = ATT_HEAD_DIM
    lay = []
    for base in (0, half):
        lay += [(0, base + i) for i in range(half)] + [(1, base + i) for i in range(half)]
        lay += [(base // half, d) for d in range(ROT_DIM, E)]
    return lay


def _attn_kernel(q_ref, k_ref, v_ref, o_ref, m_a_ref, m_b_ref, acc_a_ref, acc_b_ref, tmp_ref,
                 qs_ref, ka_ref, kb_ref, va_ref, vb_ref):
    grp = pl.program_id(2)
    L, lw = q_ref.shape
    T = ATT_BLOCK
    E = ATT_HEAD_DIM
    n_groups = len(ATT_DILATIONS)

    lane1 = lax.broadcasted_iota(jnp.int32, (1, lw), 1)
    head_of_lane = np.array([h for h, _ in _rot_lane_layout()])
    edges = np.flatnonzero(np.diff(np.concatenate([[1], head_of_lane, [1]])))
    in_a_qk = jnp.zeros((1, lw), jnp.bool_)
    for lo, hi in zip(edges[0::2], edges[1::2]):
        in_a_qk = in_a_qk | ((lane1 >= int(lo)) & (lane1 < int(hi)))
    in_a_v = lane1 < E
    qi = lax.broadcasted_iota(jnp.int32, (T, T), 0)
    kj = lax.broadcasted_iota(jnp.int32, (T, T), 1)

    S = ATT_DILATIONS[1]
    assert ATT_DILATIONS == (1, S, S * S) and T % S == 0
    Ls = L // S

    def by_s_rows(dil, r, n):
        if dil == S:
            return slice(r * Ls + n * T, r * Ls + (n + 1) * T)
        return pl.ds((r % S) * Ls + r // S, T, stride=S)

    def run_group(gi, dil):
        M = L // dil
        nblk = M // T

        def residue_major(x_ref, emit):
            if dil == S * S:
                for r4 in range(S):
                    tmp_ref[r4 * Ls:(r4 + 1) * Ls, :] = x_ref[pl.ds(r4, Ls, stride=S), :]
            for r in range(dil):
                if dil == 1:
                    xr = x_ref[...]
                elif dil == S:
                    xr = x_ref[pl.ds(r, M, stride=S), :]
                else:
                    xr = tmp_ref[by_s_rows(dil, r, 0), :]
                emit(slice(r * M, (r + 1) * M), xr)

        def emit_q(dst, xr):
            qs_ref[dst, :] = (xr * (E ** -0.5 * LOG2E)).astype(BF16)

        def emit_k(dst, xr):
            ka_ref[dst, :] = jnp.where(in_a_qk, xr, 0.0).astype(BF16)
            kb_ref[dst, :] = jnp.where(in_a_qk, 0.0, xr).astype(BF16)

        def emit_v(dst, xr):
            va_ref[dst, :] = jnp.where(in_a_v, xr, 1.0).astype(BF16)
            vb_ref[dst, :] = jnp.where(in_a_v, 1.0, xr).astype(BF16)

        if dil == 1:
            for n in range(nblk):
                for j in range(S):
                    emit_q(slice(n * T + j * (T // S), n * T + (j + 1) * (T // S)),
                           q_ref[pl.ds(n * T + j, T // S, stride=S), :])
            q_pos = S * (qi % (T // S)) + qi // (T // S)
        else:
            residue_major(q_ref, emit_q)
            q_pos = qi
        cur_ok = kj <= q_pos
        both_ok = jnp.concatenate([kj >= q_pos, cur_ok], axis=1)
        residue_major(k_ref, emit_k)
        residue_major(v_ref, emit_v)

        k_refs, v_refs = (ka_ref, kb_ref), (va_ref, vb_ref)
        m_refs, acc_refs = (m_a_ref, m_b_ref), (acc_a_ref, acc_b_ref)
        blocks = [(r, n) for r in range(dil) for n in range(nblk)]

        def load_state(ref, r, n):
            return ref[by_s_rows(dil, r, n), :]

        def store_state(ref, r, n, val):
            if dil == 1:
                for j in range(S):
                    ref[j * Ls + (T // S) * n:j * Ls + (T // S) * (n + 1), :] = (
                        val[j * (T // S):(j + 1) * (T // S), :])
            else:
                ref[by_s_rows(dil, r, n), :] = val

        def kv_rows(r, n):
            c0 = r * M + n * T
            return slice(c0 - T, c0 + T) if n > 0 else slice(c0, c0 + T)

        def scores(r, n, hh):
            c0 = r * M + n * T
            return _dot_nt(qs_ref[c0:c0 + T, :], k_refs[hh][kv_rows(r, n), :])

        def softmax(r, n, hh, s):
            s = jnp.where(both_ok if n > 0 else cur_ok, s, NEG_BIG)
            m = jnp.broadcast_to(jnp.max(s, axis=-1, keepdims=True), (T, lw))
            alpha = None
            if gi > 0:
                m_old = load_state(m_refs[hh], r, n)
                m = jnp.maximum(m, m_old)
                alpha = jnp.exp2(m_old - m)
            if gi < n_groups - 1:
                store_state(m_refs[hh], r, n, m)
            if n > 0:
                m = jnp.concatenate([m, m], axis=1)
            return jnp.exp2(s - m).astype(BF16), alpha

        pv_first = {}

        def values(r, n, hh, p, alpha):
            pv = _dot(p, v_refs[hh][kv_rows(r, n), :])
            if gi > 0:
                pv = alpha * load_state(acc_refs[hh], r, n) + pv
            if gi < n_groups - 1:
                store_state(acc_refs[hh], r, n, pv)
            elif hh == 0:
                pv_first[(r, n)] = pv / pltpu.roll(pv, E, axis=1)
            else:
                tmp_ref[by_s_rows(dil, r, n), :] = jnp.where(
                    in_a_v, pv_first.pop((r, n)), pv / pltpu.roll(pv, E, axis=1))

        s_val, p_val = {}, {}
        for step in range(len(blocks) + 3):
            if step < len(blocks):
                s_val[step] = [scores(*blocks[step], hh) for hh in range(2)]
            if 0 <= step - 1 < len(blocks):
                s_pair = s_val.pop(step - 1)
                p_val[step - 1] = [softmax(*blocks[step - 1], hh, s_pair[hh]) for hh in range(2)]
            if 0 <= step - 3 < len(blocks):
                p_pair = p_val.pop(step - 3)
                for hh in range(2):
                    values(*blocks[step - 3], hh, *p_pair[hh])

        if gi == n_groups - 1:
            for r4 in range(S):
                o_ref[pl.ds(r4, Ls, stride=S), :] = tmp_ref[r4 * Ls:(r4 + 1) * Ls, :]

    for gi, dil in enumerate(ATT_DILATIONS):
        pl.when(grp == gi)(functools.partial(run_group, gi, dil))


def _attn(qkv, B, L, gw):
    N = qkv.shape[0]
    lw = 2 * ATT_HEAD_DIM
    npair = gw // lw
    ng = len(ATT_DILATIONS)

    def spec(kind):
        return pl.BlockSpec((L, lw), lambda b, p, g: (b, kind * ng * npair + g * npair + p))

    return pl.pallas_call(
        _attn_kernel,
        out_shape=jax.ShapeDtypeStruct((N, gw), F32),
        grid=(B, npair, ng),
        in_specs=[spec(0), spec(1), spec(2)],
        out_specs=pl.BlockSpec((L, lw), lambda b, p, g: (b, p)),
        scratch_shapes=[pltpu.VMEM((L, lw), F32)] * 4
                       + [pltpu.VMEM((L, lw), F32)]
                       + [pltpu.VMEM((L, lw), BF16)] * 5,
        compiler_params=_cparams(3),
        name="dilated_attention",
    )(qkv, qkv, qkv)


def _s5_params(A_re, A_im, log_dt, B_re, B_im, C_re, C_im, Dd):
    G, P = A_re.shape
    dt = jnp.exp(log_dt)[:, None]
    mag = jnp.exp(A_re * dt)
    ab_re = mag * jnp.cos(A_im * dt)
    ab_im = mag * jnp.sin(A_im * dt)
    den = A_re * A_re + A_im * A_im
    nr, ni = ab_re - 1.0, ab_im
    c_re = (nr * A_re + ni * A_im) / den
    c_im = (ni * A_re - nr * A_im) / den
    Bb_re = c_re[..., None] * B_re - c_im[..., None] * B_im
    Bb_im = c_re[..., None] * B_im + c_im[..., None] * B_re
    hg = G // 2
    eye = jnp.eye(hg, dtype=F32)

    def bd_in(m):
        return jnp.einsum('gpc,gh->gchp', m, eye).reshape(hg * S5_GROUP, hg * P)

    def bd_out(m):
        return jnp.einsum('gcp,gh->gphc', m, eye).reshape(hg * P, hg * S5_GROUP)

    bds, cds = [], []
    for k in range(2):
        sl = slice(k * hg, (k + 1) * hg)
        bds.append(jnp.concatenate([bd_in(Bb_re[sl]), bd_in(Bb_im[sl])], axis=1).astype(BF16))
        cds.append(jnp.concatenate([bd_out(C_re[sl]), -bd_out(C_im[sl])], axis=0).astype(BF16))
    return bds[0], bds[1], ab_re.reshape(1, G * P), ab_im.reshape(1, G * P), cds[0], cds[1], Dd.reshape(1, -1)


ROPE_COLS = 32


def _rope_tables(positions):
    half = ROT_DIM // 2
    inv_freq = ROPE_THETA ** (-jnp.arange(half, dtype=F32) * 2.0 / ROT_DIM)
    ang = positions.astype(F32).reshape(-1, 1) * inv_freq
    n = ang.shape[0]
    cs = jnp.concatenate([jnp.cos(ang), jnp.sin(ang), jnp.ones((n, 1), F32),
                          jnp.zeros((n, ROPE_COLS - 2 * half - 1), F32)], axis=1)
    ex = np.zeros((ROPE_COLS, 2 * LANES), np.float32)
    for lane, (_, d) in enumerate(_rot_lane_layout()):
        if d < half:
            ex[d, lane] = 1.0
            ex[half + d, LANES + lane] = -1.0
        elif d < ROT_DIM:
            ex[d - half, lane] = 1.0
            ex[half + d - half, LANES + lane] = 1.0
        else:
            ex[2 * half, lane] = 1.0
    return cs, jnp.asarray(ex, dtype=BF16)


def _qk_lane_layout(w):
    rows, n_cols = w.shape
    E = ATT_HEAD_DIM
    wp = w.reshape(rows, n_cols // LANES, 2, E)
    lay = _rot_lane_layout()
    pieces, start = [], 0
    for l in range(1, LANES + 1):
        if l == LANES or lay[l][0] != lay[start][0] or lay[l][1] != lay[l - 1][1] + 1:
            h, d0 = lay[start]
            pieces.append(wp[:, :, h, d0:d0 + (l - start)])
            start = l
    return jnp.concatenate(pieces, axis=-1).reshape(rows, n_cols)


def kernel(x, positions, norm_mix, norm_ffn, norm_final, mix_w_in, mix_w_out, s5_A_re, s5_A_im, s5_log_dt, s5_B_re, s5_B_im, s5_C_re, s5_C_im, s5_D, s5_glu_w, s5_glu_b, hgrn_gamma, hgrn_norm, att_w_qkv, att_w_o, ffn_w_in, ffn_conv_w, ffn_conv_b, ffn_w_out):
    B, L, D = x.shape
    N = B * L
    x2 = x.reshape(N, D)
    tm = PROJ_ROW_TILE
    s5w = s5_A_re.shape[1] * S5_GROUP
    hgw = hgrn_norm.shape[1]

    ffn_cast = lambda layer, steps: ((ffn_w_in, layer, D // steps), (ffn_w_out, layer, FFN_COL_TILE))
    g_ffn, conv_b = norm_ffn[:, None, :], ffn_conv_b[:, None, :]
    g_fin = norm_final.reshape(1, D)

    proj = _norm_proj(x2, norm_mix[0:1], mix_w_in[0].astype(BF16), tm, s5w, out_dtype=BF16, name="inproj")[0]
    s5p = _s5_params(s5_A_re[0], s5_A_im[0], s5_log_dt[0], s5_B_re[0], s5_B_im[0],
                     s5_C_re[0], s5_C_im[0], s5_D[0])
    oa, w_in0, w_out0 = _s5(proj.reshape(B, L, -1), *s5p, s5_glu_w[0].astype(BF16), s5_glu_b[0:1], s5w,
                            cast=ffn_cast(0, L // S5_TIME_TILE))
    lb_all = jnp.cumsum(jax.nn.softmax(hgrn_gamma.astype(F32), axis=0), axis=0)
    ob = _hgrn(proj, lb_all[0:1], hgrn_norm[0:1], B, L, col0=s5w // hgw)
    h = _mix_ffn(x2, [oa.reshape(N, s5w), ob], mix_w_out[0].astype(BF16), 0, g_ffn, w_in0, ffn_conv_w,
                 conv_b, w_out0, g_fin, L, final_norm=False)

    gw = att_w_o.shape[1]
    n_rot_cols = 2 * len(ATT_DILATIONS) * gw
    w_qkv = att_w_qkv[0].astype(BF16)
    w_qkv = jnp.concatenate([_qk_lane_layout(w_qkv[:, :n_rot_cols]), w_qkv[:, n_rot_cols:]], axis=1)
    qkv, w_in1, w_out1 = _norm_proj(h, norm_mix[1:2], w_qkv, tm, gw, rot=_rope_tables(positions),
                                    n_rot_cols=n_rot_cols, cast=ffn_cast(1, N // tm), name="qkv_rotary")
    ao = _attn(qkv, B, L, gw)
    h = _mix_ffn(h, [ao], att_w_o[0].astype(BF16), 1, g_ffn, w_in1, ffn_conv_w, conv_b, w_out1, g_fin, L,
                 final_norm=True)
    return h.reshape(B, L, D)
```
